```python
import jax, jax.numpy as jnp
from jax import lax
import numpy as np

D_MODEL = 1024
BATCH = 8
SEQ = 8192
DEPTH = 1

N_META = 16
BLOCK = 128
PAD = BLOCK - N_META
HEAD_DIM = 64
SB_HEADS = 8
SB_WIDTH = SB_HEADS * HEAD_DIM
SWA_Q_HEADS = 16
SWA_KV_HEADS = 2
SWA_GROUP = SWA_Q_HEADS // SWA_KV_HEADS
SWA_WIDTH = SWA_Q_HEADS * HEAD_DIM
SWA_KV_WIDTH = SWA_KV_HEADS * HEAD_DIM
WINDOW = 128
ROPE_THETA = 10000.0
RMS_EPS = 1e-6
SPLITS = (SB_WIDTH, SB_WIDTH, SB_WIDTH, SWA_WIDTH, SWA_KV_WIDTH, SWA_KV_WIDTH, SB_WIDTH, SWA_WIDTH, D_MODEL, D_MODEL)
IN_COLS = sum(SPLITS)

kernel_name = "hybrid_stickbreak_swa_sink_gated"


def rms_norm(x, g):
    xf = x.astype(jnp.float32)
    y = xf * lax.rsqrt(jnp.mean(xf * xf, axis=-1, keepdims=True) + RMS_EPS)
    return (y * g.astype(jnp.float32)).astype(x.dtype)


def rope(x, pos):
    half = HEAD_DIM // 2
    inv = ROPE_THETA ** (-jnp.arange(half, dtype=jnp.float32) / half)
    ang = pos.astype(jnp.float32)[:, None] * inv[None, :]
    cos = jnp.cos(ang)[None, :, None, :]
    sin = jnp.sin(ang)[None, :, None, :]
    x1 = x[..., :half].astype(jnp.float32)
    x2 = x[..., half:].astype(jnp.float32)
    out = jnp.concatenate([x1 * cos - x2 * sin, x2 * cos + x1 * sin], axis=-1)
    return out.astype(x.dtype)


def stick_breaking_attention(q, k, v, valid):
    B, L, H, d = q.shape
    scale = d ** -0.5
    outs = []
    for blk in range(L // BLOCK):
        q0 = blk * BLOCK
        end = q0 + BLOCK
        z = jnp.einsum('bqhd,bkhd->bhqk', q[:, q0:end], k[:, :end],
                       preferred_element_type=jnp.float32) * scale
        t = q0 + jnp.arange(BLOCK)
        s = jnp.arange(end)
        mask = (s[None, :] < t[:, None]) & valid[None, :end]
        log_beta = jax.nn.log_sigmoid(z)
        log_1m = jnp.where(mask, log_beta - z, 0.0)
        rev = lax.cumsum(log_1m, axis=3, reverse=True)
        suffix = jnp.concatenate([rev[..., 1:], jnp.zeros_like(rev[..., :1])], axis=-1)
        w = jnp.where(mask, jnp.exp(log_beta + suffix), 0.0)
        outs.append(jnp.einsum('bhqk,bkhd->bqhd', w.astype(v.dtype), v[:, :end]))
    return jnp.concatenate(outs, axis=1)


def sliding_window_sink_attention(q, k, v, sinks, valid):
    B, L, Hq, d = q.shape
    nb = L // BLOCK
    scale = d ** -0.5
    qb = q.reshape(B, nb, BLOCK, SWA_KV_HEADS, SWA_GROUP, d)

    def band(t):
        tb = t.reshape(B, nb, BLOCK, SWA_KV_HEADS, d)
        prev = jnp.pad(tb[:, :-1], ((0, 0), (1, 0), (0, 0), (0, 0), (0, 0)))
        return jnp.concatenate([prev, tb], axis=2)

    kb, vb = band(k), band(v)
    vblk = valid.reshape(nb, BLOCK)
    kvalid = jnp.concatenate([jnp.pad(vblk[:-1], ((1, 0), (0, 0)), constant_values=False), vblk], axis=1)
    scores = jnp.einsum('bnqhgd,bnkhd->bnhgqk', qb, kb,
                        preferred_element_type=jnp.float32) * scale
    diff = (BLOCK + jnp.arange(BLOCK))[:, None] - jnp.arange(2 * BLOCK)[None, :]
    mask = ((diff >= 0) & (diff < WINDOW))[None] & kvalid[:, None, :]
    scores = jnp.where(mask[None, :, None, None], scores, -jnp.inf)
    sink = sinks.astype(jnp.float32).reshape(SWA_KV_HEADS, SWA_GROUP)[None, None, :, :, None, None]
    sink = jnp.broadcast_to(sink, scores.shape[:-1] + (1,))
    probs = jax.nn.softmax(jnp.concatenate([scores, sink], axis=-1), axis=-1)[..., :-1]
    o = jnp.einsum('bnhgqk,bnkhd->bnqhgd', probs.astype(v.dtype), vb)
    return o.reshape(B, L, Hq * d)


def _fwd_setup_inputs(seed: int = 0) -> dict:
    key = jax.random.key(seed)
    ks = jax.random.split(key, 10)
    f32 = jnp.float32
    x = jax.random.normal(ks[0], (BATCH, SEQ, D_MODEL), f32)
    meta_tokens = jax.random.normal(ks[1], (N_META, D_MODEL), f32)
    norm_gain = 1.0 + 0.02 * jax.random.normal(ks[2], (DEPTH, D_MODEL), f32)
    w_in = jax.random.normal(ks[3], (DEPTH, D_MODEL, IN_COLS), f32) * D_MODEL ** -0.5
    w_branch_sb = jax.random.normal(ks[4], (DEPTH, SB_WIDTH, D_MODEL), f32) * SB_WIDTH ** -0.5
    w_branch_swa = jax.random.normal(ks[5], (DEPTH, SWA_WIDTH, D_MODEL), f32) * SWA_WIDTH ** -0.5
    w_out = jax.random.normal(ks[6], (DEPTH, D_MODEL, D_MODEL), f32) * D_MODEL ** -0.5
    attn_sinks = jax.random.normal(ks[7], (DEPTH, SWA_Q_HEADS), f32)
    final_norm_gain = 1.0 + 0.02 * jax.random.normal(ks[8], (D_MODEL,), f32)
    return {"x": x, "meta_tokens": meta_tokens, "norm_gain": norm_gain, "w_in": w_in,
            "w_branch_sb": w_branch_sb, "w_branch_swa": w_branch_swa, "w_out": w_out,
            "attn_sinks": attn_sinks, "final_norm_gain": final_norm_gain}


def _fwd_reference(x, meta_tokens, norm_gain, w_in, w_branch_sb, w_branch_swa, w_out, attn_sinks, final_norm_gain):
    B = x.shape[0]
    meta = jnp.broadcast_to(meta_tokens[None].astype(x.dtype), (B, N_META, D_MODEL))
    pad = jnp.zeros((B, PAD, D_MODEL), x.dtype)
    h = jnp.concatenate([pad, meta, x], axis=1)
    L = h.shape[1]
    idx = jnp.arange(L)
    valid = idx >= PAD
    pos = idx - PAD
    offsets = [int(o) for o in np.cumsum(SPLITS)[:-1]]
    for l in range(DEPTH):
        xn = rms_norm(h, norm_gain[l])
        proj = xn @ w_in[l]
        sb_q, sb_k, sb_v, sw_q, sw_k, sw_v, sb_z, sw_z, g_sb, g_sw = jnp.split(proj, offsets, axis=-1)
        o_sb = stick_breaking_attention(sb_q.reshape(B, L, SB_HEADS, HEAD_DIM),
                                        sb_k.reshape(B, L, SB_HEADS, HEAD_DIM),
                                        sb_v.reshape(B, L, SB_HEADS, HEAD_DIM), valid).reshape(B, L, SB_WIDTH)
        q = rope(sw_q.reshape(B, L, SWA_Q_HEADS, HEAD_DIM), pos)
        k = rope(sw_k.reshape(B, L, SWA_KV_HEADS, HEAD_DIM), pos)
        o_sw = sliding_window_sink_attention(q, k, sw_v.reshape(B, L, SWA_KV_HEADS, HEAD_DIM),
                                             attn_sinks[l], valid)
        y_sb = (o_sb * jax.nn.silu(sb_z)) @ w_branch_sb[l]
        y_sw = (o_sw * jax.nn.silu(sw_z)) @ w_branch_swa[l]
        merged = jax.nn.sigmoid(g_sb) * y_sb + jax.nn.sigmoid(g_sw) * y_sw
        h = h + merged @ w_out[l]
    return rms_norm(h, final_norm_gain)[:, BLOCK:]


import jax as _jax
import jax.numpy as _jnp

TWIN_FORMAT = 'train_step'
FWD_PARAMS = ['x', 'meta_tokens', 'norm_gain', 'w_in', 'w_branch_sb', 'w_branch_swa', 'w_out', 'attn_sinks', 'final_norm_gain']
TWIN_WEIGHTS = ['meta_tokens', 'norm_gain', 'w_in', 'w_branch_sb', 'w_branch_swa', 'w_out', 'attn_sinks', 'final_norm_gain']
TWIN_DIFF_INPUT = 'x'
TWIN_INPUTS = ['x', 'meta_tokens', 'norm_gain', 'w_in', 'w_branch_sb', 'w_branch_swa', 'w_out', 'attn_sinks', 'final_norm_gain', 'loss_target', 'm_meta_tokens', 'm_norm_gain', 'm_w_in', 'm_w_branch_sb', 'm_w_branch_swa', 'm_w_out', 'm_attn_sinks', 'm_final_norm_gain', 'v_meta_tokens', 'v_norm_gain', 'v_w_in', 'v_w_branch_sb', 'v_w_branch_swa', 'v_w_out', 'v_attn_sinks', 'v_final_norm_gain']
TWIN_OUTPUTS = ['loss', 'grad_x', 'grad_meta_tokens', 'grad_norm_gain', 'grad_w_in', 'grad_w_branch_sb', 'grad_w_branch_swa', 'grad_w_out', 'grad_attn_sinks', 'grad_final_norm_gain', 'delta_meta_tokens', 'delta_norm_gain', 'delta_w_in', 'delta_w_branch_sb', 'delta_w_branch_swa', 'delta_w_out', 'delta_attn_sinks', 'delta_final_norm_gain', 'new_m_meta_tokens', 'new_m_norm_gain', 'new_m_w_in', 'new_m_w_branch_sb', 'new_m_w_branch_swa', 'new_m_w_out', 'new_m_attn_sinks', 'new_m_final_norm_gain', 'new_v_meta_tokens', 'new_v_norm_gain', 'new_v_w_in', 'new_v_w_branch_sb', 'new_v_w_branch_swa', 'new_v_w_out', 'new_v_attn_sinks', 'new_v_final_norm_gain']
TWIN_LEAF_KINDS = {'loss': 'loss', 'grad_x': 'grad_x', 'grad_meta_tokens': 'grad_w', 'grad_norm_gain': 'grad_w', 'grad_w_in': 'grad_w', 'grad_w_branch_sb': 'grad_w', 'grad_w_branch_swa': 'grad_w', 'grad_w_out': 'grad_w', 'grad_attn_sinks': 'grad_w', 'grad_final_norm_gain': 'grad_w', 'delta_meta_tokens': 'delta_w', 'delta_norm_gain': 'delta_w', 'delta_w_in': 'delta_w', 'delta_w_branch_sb': 'delta_w', 'delta_w_branch_swa': 'delta_w', 'delta_w_out': 'delta_w', 'delta_attn_sinks': 'delta_w', 'delta_final_norm_gain': 'delta_w', 'new_m_meta_tokens': 'new_m', 'new_m_norm_gain': 'new_m', 'new_m_w_in': 'new_m', 'new_m_w_branch_sb': 'new_m', 'new_m_w_branch_swa': 'new_m', 'new_m_w_out': 'new_m', 'new_m_attn_sinks': 'new_m', 'new_m_final_norm_gain': 'new_m', 'new_v_meta_tokens': 'new_v', 'new_v_norm_gain': 'new_v', 'new_v_w_in': 'new_v', 'new_v_w_branch_sb': 'new_v', 'new_v_w_branch_swa': 'new_v', 'new_v_w_out': 'new_v', 'new_v_attn_sinks': 'new_v', 'new_v_final_norm_gain': 'new_v'}


def _forward(args):
    return _fwd_reference(*[args[k] for k in FWD_PARAMS])


def _output_shape():
    def fwd():
        inp = _fwd_setup_inputs(0)
        return _fwd_reference(*[inp[k] for k in FWD_PARAMS])
    out = _jax.eval_shape(fwd)
    return out.shape, out.dtype

N_MICROBATCH = 1
ADAM_LR = 0.001
ADAM_B1 = 0.9
ADAM_B2 = 0.999
ADAM_EPS = 1e-08
ADAM_WD = 0.01
ADAM_STEP = 10
PER_EXAMPLE_BATCH_AXIS = {'x': 0, 'loss_target': 0}
SHARED_INPUTS = []
_WEIGHT_DTYPES = {'meta_tokens': _jnp.float32, 'norm_gain': _jnp.float32, 'w_in': _jnp.float32, 'w_branch_sb': _jnp.float32, 'w_branch_swa': _jnp.float32, 'w_out': _jnp.float32, 'attn_sinks': _jnp.float32, 'final_norm_gain': _jnp.float32}
MOMENT_SCALE = {'meta_tokens': 2.434593e-03, 'norm_gain': 8.496360e-02, 'w_in': 3.483767e-02, 'w_branch_sb': 5.099493e-02, 'w_branch_swa': 1.397703e-02, 'w_out': 5.272350e-02, 'attn_sinks': 2.078631e-03, 'final_norm_gain': 6.399925e+01}


def _to_microbatches(a, axis):
    t = _jnp.moveaxis(a, axis, 0)
    t = t.reshape((N_MICROBATCH, t.shape[0] // N_MICROBATCH) + t.shape[1:])
    return _jnp.moveaxis(t, 1, axis + 1)


def setup_inputs(seed: int = 0) -> dict:
    inp = _fwd_setup_inputs(seed)
    key = _jax.random.fold_in(_jax.random.key(seed), 7919)
    shape, _ = _output_shape()
    out = dict(inp)
    out["loss_target"] = _jax.random.normal(_jax.random.fold_in(key, 0), shape, _jnp.float32)
    for i, name in enumerate(TWIN_WEIGHTS):
        w = inp[name].astype(_jnp.float32)
        if MOMENT_SCALE is None:
            s = _jnp.sqrt(_jnp.mean(_jnp.square(w)) + 1e-30)
        else:
            s = MOMENT_SCALE[name]
        km, kv = _jax.random.split(_jax.random.fold_in(key, i + 1))
        out[name] = w
        out["m_" + name] = s * _jax.random.normal(km, w.shape, _jnp.float32)
        out["v_" + name] = (s * s) * _jax.random.uniform(kv, w.shape, _jnp.float32, 0.5, 1.5)
    if N_MICROBATCH > 1:
        for name, axis in PER_EXAMPLE_BATCH_AXIS.items():
            out[name] = _to_microbatches(out[name], axis)
    return {'x': out['x'], 'meta_tokens': out['meta_tokens'], 'norm_gain': out['norm_gain'], 'w_in': out['w_in'], 'w_branch_sb': out['w_branch_sb'], 'w_branch_swa': out['w_branch_swa'], 'w_out': out['w_out'], 'attn_sinks': out['attn_sinks'], 'final_norm_gain': out['final_norm_gain'], 'loss_target': out['loss_target'], 'm_meta_tokens': out['m_meta_tokens'], 'm_norm_gain': out['m_norm_gain'], 'm_w_in': out['m_w_in'], 'm_w_branch_sb': out['m_w_branch_sb'], 'm_w_branch_swa': out['m_w_branch_swa'], 'm_w_out': out['m_w_out'], 'm_attn_sinks': out['m_attn_sinks'], 'm_final_norm_gain': out['m_final_norm_gain'], 'v_meta_tokens': out['v_meta_tokens'], 'v_norm_gain': out['v_norm_gain'], 'v_w_in': out['v_w_in'], 'v_w_branch_sb': out['v_w_branch_sb'], 'v_w_branch_swa': out['v_w_branch_swa'], 'v_w_out': out['v_w_out'], 'v_attn_sinks': out['v_attn_sinks'], 'v_final_norm_gain': out['v_final_norm_gain']}


def _loss(weights, diff, rest, loss_target):
    with _jax.named_scope("forward"):
        args = {**rest, TWIN_DIFF_INPUT: diff, **{k: w.astype(_WEIGHT_DTYPES[k]) for k, w in weights.items()}}
        y = _forward(args)
    with _jax.named_scope("loss_head"):
        err = _jnp.square(y.astype(_jnp.float32) - loss_target)
        return 0.5 * _jnp.sum(_jnp.mean(err, axis=-1)) if err.ndim else 0.5 * err


def _adamw(w, g, m, v):
    m = ADAM_B1 * m + (1.0 - ADAM_B1) * g
    v = ADAM_B2 * v + (1.0 - ADAM_B2) * _jnp.square(g)
    m_hat = m / (1.0 - ADAM_B1 ** ADAM_STEP)
    v_hat = v / (1.0 - ADAM_B2 ** ADAM_STEP)
    delta = -ADAM_LR * (m_hat / (_jnp.sqrt(v_hat) + ADAM_EPS) + ADAM_WD * w)
    return delta, m, v


def reference(x, meta_tokens, norm_gain, w_in, w_branch_sb, w_branch_swa, w_out, attn_sinks, final_norm_gain, loss_target, m_meta_tokens, m_norm_gain, m_w_in, m_w_branch_sb, m_w_branch_swa, m_w_out, m_attn_sinks, m_final_norm_gain, v_meta_tokens, v_norm_gain, v_w_in, v_w_branch_sb, v_w_branch_swa, v_w_out, v_attn_sinks, v_final_norm_gain):
    given = dict(x=x, meta_tokens=meta_tokens, norm_gain=norm_gain, w_in=w_in, w_branch_sb=w_branch_sb, w_branch_swa=w_branch_swa, w_out=w_out, attn_sinks=attn_sinks, final_norm_gain=final_norm_gain, loss_target=loss_target, m_meta_tokens=m_meta_tokens, m_norm_gain=m_norm_gain, m_w_in=m_w_in, m_w_branch_sb=m_w_branch_sb, m_w_branch_swa=m_w_branch_swa, m_w_out=m_w_out, m_attn_sinks=m_attn_sinks, m_final_norm_gain=m_final_norm_gain, v_meta_tokens=v_meta_tokens, v_norm_gain=v_norm_gain, v_w_in=v_w_in, v_w_branch_sb=v_w_branch_sb, v_w_branch_swa=v_w_branch_swa, v_w_out=v_w_out, v_attn_sinks=v_attn_sinks, v_final_norm_gain=v_final_norm_gain)
    weights = {n: given[n] for n in TWIN_WEIGHTS}
    shared = {n: given[n] for n in SHARED_INPUTS}
    per_example = {n: given[n] for n in ['x']}
    grad_fn = _jax.value_and_grad(_loss, argnums=(0, 1))

    def one_microbatch(ex, loss_target):
        ex = dict(ex)
        diff = ex.pop(TWIN_DIFF_INPUT)
        return grad_fn(weights, diff, {**shared, **ex}, loss_target)

    if N_MICROBATCH == 1:
        loss, (grad_w, grad_x) = one_microbatch(per_example, given["loss_target"])
    else:
        def body(carry, xs):
            loss_sum, grad_sum = carry
            l_k, (gw_k, gx_k) = one_microbatch(xs[0], xs[1])
            with _jax.named_scope("update"):
                return (loss_sum + l_k, _jax.tree.map(_jnp.add, grad_sum, gw_k)), gx_k

        init = (_jnp.zeros((), _jnp.float32), _jax.tree.map(_jnp.zeros_like, weights))
        (loss, grad_w), grad_x = _jax.lax.scan(body, init, (per_example, given["loss_target"]))
    with _jax.named_scope("update"):
        delta_w, new_m, new_v = {}, {}, {}
        for n in TWIN_WEIGHTS:
            delta_w[n], new_m[n], new_v[n] = _adamw(weights[n], grad_w[n], given["m_" + n], given["v_" + n])
    return (loss, grad_x, *[grad_w[n] for n in TWIN_WEIGHTS], *[delta_w[n] for n in TWIN_WEIGHTS],
            *[new_m[n] for n in TWIN_WEIGHTS], *[new_v[n] for n in TWIN_WEIGHTS])
```

```python
import functools

import jax
import jax.numpy as jnp
from jax import lax
from jax.experimental import pallas as pl
from jax.experimental.pallas import tpu as pltpu

F32 = jnp.float32
BF16 = jnp.bfloat16

D_MODEL = 1024
BLOCK = 128
N_META = 16
PAD = BLOCK - N_META
HEAD_DIM = 64
SB_W = 512
SW_W = 1024
KV_W = 128
QKV_W = SW_W + 3 * SB_W + 2 * KV_W
GATE_W = 2 * D_MODEL + SW_W + SB_W
IN_COLS = QKV_W + GATE_W
ROPE_THETA = 10000.0
RMS_EPS = 1e-6
Q_SCALE = HEAD_DIM ** -0.5

ADAM_LR = 0.001
ADAM_B1 = 0.9
ADAM_B2 = 0.999
ADAM_EPS = 1e-08
ADAM_WD = 0.01
ADAM_STEP = 10

N_DEV = 8
N_CHIP = 4
MESH = pl.DeviceIdType.MESH
VMEM_LIMIT = 56 * 1024 * 1024
ANY = pl.BlockSpec(memory_space=pl.ANY)


def _cparams(*sem):
    return pltpu.CompilerParams(dimension_semantics=sem, vmem_limit_bytes=VMEM_LIMIT)


def _row_tile(n, pref):
    best = None
    for t in range(16, pref + 1, 16):
        if n % t == 0:
            best = t
    assert best is not None, (n, pref)
    return best


def _col_tile(n, pref):
    best = None
    for t in range(128, pref + 1, 128):
        if n % t == 0:
            best = t
    assert best is not None, (n, pref)
    return best


def _dot(a, b):
    return jnp.dot(a, b, preferred_element_type=F32)


def _dot_nt(a, b):
    return lax.dot_general(a, b, (((1,), (1,)), ((), ())), preferred_element_type=F32)


def _dot_tn(a, b):
    return lax.dot_general(a, b, (((0,), (0,)), ((), ())), preferred_element_type=F32)


def _lane_lo(shape):
    return lax.broadcasted_iota(jnp.int32, shape, len(shape) - 1) % BLOCK < HEAD_DIM


def _mm(pairs, out_dtype, name, tm_pref=640, tn_pref=1792):
    M = pairs[0][0].shape[0]
    N = pairs[0][1].shape[1]
    tm = _row_tile(M, tm_pref)
    tn = _col_tile(N, tn_pref)
    n_pairs = len(pairs)

    def body(*refs):
        o_ref = refs[-1]
        acc = None
        for p in range(n_pairs):
            d = _dot(refs[2 * p][...].astype(BF16), refs[2 * p + 1][...])
            acc = d if acc is None else acc + d
        o_ref[...] = acc.astype(out_dtype)

    in_specs, args = [], []
    for a, b in pairs:
        k = a.shape[1]
        in_specs += [pl.BlockSpec((tm, k), lambda n, m: (m, 0)), pl.BlockSpec((k, tn), lambda n, m: (0, n))]
        args += [a, b]
    return pl.pallas_call(
        body, name=name, grid=(N // tn, M // tm), in_specs=in_specs,
        out_specs=pl.BlockSpec((tm, tn), lambda n, m: (m, n)),
        out_shape=jax.ShapeDtypeStruct((M, N), out_dtype),
        compiler_params=_cparams("parallel", "arbitrary"),
    )(*args)


def _mm_tn(a, b, name, tn_pref=1792, tl_pref=640):
    L, M = a.shape
    N = b.shape[1]
    tn = _col_tile(N, tn_pref)
    tl = _row_tile(L, tl_pref)

    def body(a_ref, b_ref, o_ref):
        @pl.when(pl.program_id(1) == 0)
        def _():
            o_ref[...] = jnp.zeros_like(o_ref)
        o_ref[...] += _dot_tn(a_ref[...].astype(BF16), b_ref[...].astype(BF16))

    return pl.pallas_call(
        body, name=name, grid=(N // tn, L // tl),
        in_specs=[pl.BlockSpec((tl, M), lambda n, l: (l, 0)), pl.BlockSpec((tl, tn), lambda n, l: (l, n))],
        out_specs=pl.BlockSpec((M, tn), lambda n, l: (0, n)),
        out_shape=jax.ShapeDtypeStruct((M, N), F32),
        compiler_params=_cparams("parallel", "arbitrary"),
    )(a, b)


def _norm_fwd(h0, gain):
    L = h0.shape[0]
    tm = _row_tile(L, 640)

    def body(h_ref, g_ref, o_ref):
        h = h_ref[...]
        r = lax.rsqrt(jnp.mean(h * h, axis=-1, keepdims=True) + RMS_EPS)
        o_ref[...] = ((h * r) * g_ref[...]).astype(BF16)

    return pl.pallas_call(
        body, name="norm_fwd", grid=(L // tm,),
        in_specs=[pl.BlockSpec((tm, D_MODEL), lambda i: (i, 0)), pl.BlockSpec((1, D_MODEL), lambda i: (0, 0))],
        out_specs=pl.BlockSpec((tm, D_MODEL), lambda i: (i, 0)),
        out_shape=jax.ShapeDtypeStruct((L, D_MODEL), BF16),
        compiler_params=_cparams("parallel"),
    )(h0, gain)


def _rope_partner(x, lo32):
    return jnp.where(lo32, pltpu.roll(x, 96, 1), pltpu.roll(x, 32, 1))


def _prep_qkv(qkv, cos_t, sin_t):
    L = qkv.shape[0]
    tm = _row_tile(L, 320)

    def body(x_ref, c_ref, s_ref, swq, sbq, klo, khi, vlo, vhi, swk4, swv4):
        C = c_ref[...]
        S = s_ref[...]
        lane = lax.broadcasted_iota(jnp.int32, (tm, BLOCK), 1)
        lo32 = lane % HEAD_DIM < HEAD_DIM // 2
        lo = lane < HEAD_DIM

        def rope(x):
            return x * C + _rope_partner(x, lo32) * S

        for m in range(SW_W // BLOCK):
            sl = slice(m * BLOCK, (m + 1) * BLOCK)
            swq[:, sl] = (rope(x_ref[:, sl]) * Q_SCALE).astype(BF16)
        for m in range(SB_W // BLOCK):
            sl = slice(m * BLOCK, (m + 1) * BLOCK)
            sbq[:, sl] = (x_ref[:, SW_W + m * BLOCK:SW_W + (m + 1) * BLOCK] * Q_SCALE).astype(BF16)
            k = x_ref[:, SW_W + SB_W + m * BLOCK:SW_W + SB_W + (m + 1) * BLOCK]
            v = x_ref[:, SW_W + 2 * SB_W + m * BLOCK:SW_W + 2 * SB_W + (m + 1) * BLOCK]
            klo[:, sl] = jnp.where(lo, k, 0.0).astype(BF16)
            khi[:, sl] = jnp.where(lo, 0.0, k).astype(BF16)
            vlo[:, sl] = jnp.where(lo, v, 0.0).astype(BF16)
            vhi[:, sl] = jnp.where(lo, 0.0, v).astype(BF16)
        base = SW_W + 3 * SB_W
        kx = rope(x_ref[:, base:base + KV_W])
        vx = x_ref[:, base + KV_W:base + 2 * KV_W]
        for src, dst in ((kx, swk4), (vx, swv4)):
            sw = pltpu.roll(src, HEAD_DIM, 1)
            dst[:, 0:128] = jnp.where(lo, src, 0.0).astype(BF16)
            dst[:, 128:256] = jnp.where(lo, 0.0, sw).astype(BF16)
            dst[:, 256:384] = jnp.where(lo, sw, 0.0).astype(BF16)
            dst[:, 384:512] = jnp.where(lo, 0.0, src).astype(BF16)

    row = lambda w: pl.BlockSpec((tm, w), lambda i: (i, 0))
    shp = lambda w: jax.ShapeDtypeStruct((L, w), BF16)
    return pl.pallas_call(
        body, name="prep_qkv", grid=(L // tm,),
        in_specs=[row(QKV_W), row(BLOCK), row(BLOCK)],
        out_specs=[row(SW_W)] + [row(SB_W)] * 7,
        out_shape=[shp(SW_W)] + [shp(SB_W)] * 7,
        compiler_params=_cparams("parallel"),
    )(qkv, cos_t, sin_t)


def _cumsum_matrix():
    r = lax.broadcasted_iota(jnp.int32, (BLOCK, BLOCK), 0)
    c = lax.broadcasted_iota(jnp.int32, (BLOCK, BLOCK), 1)
    return jnp.where(r >= c, 1.0, 0.0).astype(BF16)


def _rev_cumsum(x, U):
    hi = x.astype(BF16)
    lo = (x - hi.astype(F32)).astype(BF16)
    return _dot(hi, U) + _dot(lo, U)


def _sb_tile(q, k, mask, U, carry):
    z = _dot_nt(q, k)
    lb = jnp.minimum(z, 0.0) - jnp.log(1.0 + jnp.exp(-jnp.abs(z)))
    l1m = jnp.where(mask, lb - z, 0.0)
    c = _rev_cumsum(l1m, U)
    w = jnp.where(mask, jnp.exp(lb + (c - l1m + carry)), 0.0)
    return w, lb, c


def _sb_mask(i, j):
    t = i * BLOCK + lax.broadcasted_iota(jnp.int32, (BLOCK, BLOCK), 0)
    s = j * BLOCK + lax.broadcasted_iota(jnp.int32, (BLOCK, BLOCK), 1)
    return (s < t) & (s >= PAD)


def _sb_fwd(sbq, klo, khi, vlo, vhi):
    L = sbq.shape[0]
    nq = L // BLOCK

    def body(q_ref, klo_ref, khi_ref, vlo_ref, vhi_ref, o_ref):
        i = pl.program_id(1)
        q = q_ref[...]
        U = _cumsum_matrix()

        def step(jj, carry):
            acc, c0, c1 = carry
            j = i - jj
            rows = pl.ds(pl.multiple_of(j * BLOCK, BLOCK), BLOCK)
            mask = _sb_mask(i, j)
            w0, _, cs0 = _sb_tile(q, klo_ref[rows, :], mask, U, c0)
            w1, _, cs1 = _sb_tile(q, khi_ref[rows, :], mask, U, c1)
            acc = acc + _dot(w0.astype(BF16), vlo_ref[rows, :]) + _dot(w1.astype(BF16), vhi_ref[rows, :])
            return acc, c0 + cs0[:, 0:1], c1 + cs1[:, 0:1]

        zc = jnp.zeros((BLOCK, 1), F32)
        acc, _, _ = lax.fori_loop(0, i + 1, step, (jnp.zeros((BLOCK, BLOCK), F32), zc, zc))
        o_ref[...] = acc

    qspec = pl.BlockSpec((BLOCK, BLOCK), lambda p, i: (i, p))
    kvspec = pl.BlockSpec((L, BLOCK), lambda p, i: (0, p))
    return pl.pallas_call(
        body, name="sb_fwd", grid=(SB_W // BLOCK, nq),
        in_specs=[qspec, kvspec, kvspec, kvspec, kvspec], out_specs=qspec,
        out_shape=jax.ShapeDtypeStruct((L, SB_W), F32),
        compiler_params=_cparams("parallel", "arbitrary"),
    )(sbq, klo, khi, vlo, vhi)


def _sb_bwd(sbq, klo, khi, vlo, vhi, do_sb):
    L = sbq.shape[0]
    nq = L // BLOCK

    def body(q_ref, klo_ref, khi_ref, vlo_ref, vhi_ref, do_ref, dq_ref, dk_ref, dv_ref, g_scr, beta_scr):
        i = pl.program_id(1)

        @pl.when(i == 0)
        def _():
            dk_ref[...] = jnp.zeros_like(dk_ref)
            dv_ref[...] = jnp.zeros_like(dv_ref)

        lo = _lane_lo((BLOCK, BLOCK))
        q = q_ref[...]
        qf = q.astype(F32)
        q_lo = jnp.where(lo, qf, 0.0).astype(BF16)
        q_hi = jnp.where(lo, 0.0, qf).astype(BF16)
        do = do_ref[...]
        do_b = do.astype(BF16)
        do_lo = jnp.where(lo, do, 0.0).astype(BF16)
        do_hi = jnp.where(lo, 0.0, do).astype(BF16)
        U = _cumsum_matrix()
        r_i = lax.broadcasted_iota(jnp.int32, (BLOCK, BLOCK), 0)
        c_i = lax.broadcasted_iota(jnp.int32, (BLOCK, BLOCK), 1)
        U_fwd = jnp.where(r_i <= c_i, 1.0, 0.0).astype(BF16)

        def down(jj, carry):
            cl0, cl1 = carry
            j = i - jj
            rows = pl.ds(pl.multiple_of(j * BLOCK, BLOCK), BLOCK)
            mask = _sb_mask(i, j)
            ws = []
            for h, (k_ref, v_ref, cl) in enumerate(((klo_ref, vlo_ref, cl0), (khi_ref, vhi_ref, cl1))):
                w, lb, cs = _sb_tile(q, k_ref[rows, :], mask, U, cl)
                g_scr[h, j] = _dot_nt(do_b, v_ref[rows, :]) * w
                beta_scr[h, j] = jnp.exp(lb).astype(BF16)
                ws.append((w.astype(BF16), cl + cs[:, 0:1]))
            dv_ref[rows, :] += _dot_tn(ws[0][0], do_lo) + _dot_tn(ws[1][0], do_hi)
            return ws[0][1], ws[1][1]

        zc = jnp.zeros((BLOCK, 1), F32)
        lax.fori_loop(0, i + 1, down, (zc, zc))

        def up(j, carry):
            dq, cg0, cg1 = carry
            rows = pl.ds(pl.multiple_of(j * BLOCK, BLOCK), BLOCK)
            mask = _sb_mask(i, j)
            dzs = []
            for h, cg in enumerate((cg0, cg1)):
                g = g_scr[h, j]
                beta = beta_scr[h, j].astype(F32)
                hi = g.astype(BF16)
                upto = _dot(hi, U_fwd) + _dot((g - hi.astype(F32)).astype(BF16), U_fwd)
                before = upto - g + cg
                dz = jnp.where(mask, g * (1.0 - beta) - before * beta, 0.0).astype(BF16)
                dzs.append((dz, cg + upto[:, BLOCK - 1:BLOCK]))
            k0, k1 = klo_ref[rows, :], khi_ref[rows, :]
            dq = dq + _dot(dzs[0][0], k0) + _dot(dzs[1][0], k1)
            dk_ref[rows, :] += _dot_tn(dzs[0][0], q_lo) + _dot_tn(dzs[1][0], q_hi)
            return dq, dzs[0][1], dzs[1][1]

        dq, _, _ = lax.fori_loop(0, i + 1, up, (jnp.zeros((BLOCK, BLOCK), F32), zc, zc))
        dq_ref[...] = dq

    qspec = pl.BlockSpec((BLOCK, BLOCK), lambda p, i: (i, p))
    kvspec = pl.BlockSpec((L, BLOCK), lambda p, i: (0, p))
    shp = jax.ShapeDtypeStruct((L, SB_W), F32)
    return pl.pallas_call(
        body, name="sb_bwd", grid=(SB_W // BLOCK, nq),
        in_specs=[qspec, kvspec, kvspec, kvspec, kvspec, qspec],
        out_specs=[qspec, kvspec, kvspec], out_shape=[shp, shp, shp],
        scratch_shapes=[pltpu.VMEM((2, nq, BLOCK, BLOCK), F32), pltpu.VMEM((2, nq, BLOCK, BLOCK), BF16)],
        compiler_params=_cparams("parallel", "arbitrary"),
    )(sbq, klo, khi, vlo, vhi, do_sb)


def _swa_mask(i):
    t = lax.broadcasted_iota(jnp.int32, (BLOCK, 2 * BLOCK), 0)
    s = lax.broadcasted_iota(jnp.int32, (BLOCK, 2 * BLOCK), 1)
    diff = BLOCK + t - s
    return (diff >= 0) & (diff < BLOCK) & ((i - 1) * BLOCK + s >= PAD)


def _swa_probs(q, kk, mask, sink):
    sc = jnp.where(mask, _dot_nt(q, kk), -jnp.inf)
    mx = jnp.maximum(jnp.max(sc, axis=1, keepdims=True), sink)
    e = jnp.exp(sc - mx)
    es = jnp.exp(sink - mx)
    inv = 1.0 / (jnp.sum(e, axis=1, keepdims=True) + es)
    return e * inv, es * inv


def _swa_specs(L):
    prev = lambda w: pl.BlockSpec((BLOCK, w), lambda i: (jnp.maximum(i - 1, 0), 0))
    cur = lambda w: pl.BlockSpec((BLOCK, w), lambda i: (i, 0))
    sink = pl.BlockSpec(memory_space=pltpu.SMEM)
    return [cur(SW_W), prev(SB_W), cur(SB_W), prev(SB_W), cur(SB_W), sink]


def _swa_fwd(swq, swk4, swv4, sinks):
    L = swq.shape[0]

    def body(q_ref, kp_ref, kc_ref, vp_ref, vc_ref, sink_ref, o_ref):
        mask = _swa_mask(pl.program_id(0))
        k2 = jnp.concatenate([kp_ref[...], kc_ref[...]], axis=0)
        v2 = jnp.concatenate([vp_ref[...], vc_ref[...]], axis=0)
        for m in range(SW_W // BLOCK):
            g = m // 4
            sl = slice(m * BLOCK, (m + 1) * BLOCK)
            q = q_ref[:, sl]
            acc = jnp.zeros((BLOCK, BLOCK), F32)
            for par in range(2):
                kv = slice((2 * g + par) * BLOCK, (2 * g + par + 1) * BLOCK)
                p, _ = _swa_probs(q, k2[:, kv], mask, sink_ref[0, 2 * m + par])
                acc = acc + _dot(p.astype(BF16), v2[:, kv])
            o_ref[:, sl] = acc

    return pl.pallas_call(
        body, name="swa_fwd", grid=(L // BLOCK,), in_specs=_swa_specs(L),
        out_specs=pl.BlockSpec((BLOCK, SW_W), lambda i: (i, 0)),
        out_shape=jax.ShapeDtypeStruct((L, SW_W), F32),
        compiler_params=_cparams("parallel"),
    )(swq, swk4, swk4, swv4, swv4, sinks)


def _swa_bwd(swq, swk4, swv4, sinks, do_sw):
    L = swq.shape[0]

    def body(q_ref, kp_ref, kc_ref, vp_ref, vc_ref, sink_ref, do_ref, dq_ref, dkp_ref, dkc_ref, dvp_ref, dvc_ref, ds_ref):
        i = pl.program_id(0)

        @pl.when(i == 0)
        def _():
            ds_ref[...] = jnp.zeros_like(ds_ref)

        mask = _swa_mask(i)
        k2 = jnp.concatenate([kp_ref[...], kc_ref[...]], axis=0)
        v2 = jnp.concatenate([vp_ref[...], vc_ref[...]], axis=0)
        lane = lax.broadcasted_iota(jnp.int32, (8, BLOCK), 1)
        dsink = jnp.zeros((8, BLOCK), F32)
        for g in range(2):
            dkk = [jnp.zeros((2 * BLOCK, BLOCK), F32) for _ in range(2)]
            dvv = [jnp.zeros((2 * BLOCK, BLOCK), F32) for _ in range(2)]
            for m in range(4 * g, 4 * g + 4):
                sl = slice(m * BLOCK, (m + 1) * BLOCK)
                q = q_ref[:, sl]
                do_b = do_ref[:, sl].astype(BF16)
                dq = jnp.zeros((BLOCK, BLOCK), F32)
                for par in range(2):
                    kv = slice((2 * g + par) * BLOCK, (2 * g + par + 1) * BLOCK)
                    kk = k2[:, kv]
                    p, ps = _swa_probs(q, kk, mask, sink_ref[0, 2 * m + par])
                    dp = _dot_nt(do_b, v2[:, kv])
                    dd = jnp.sum(p * dp, axis=1, keepdims=True)
                    ds = (p * (dp - dd)).astype(BF16)
                    dsink = dsink - jnp.where(lane == 2 * m + par, jnp.sum(ps * dd), 0.0)
                    dq = dq + _dot(ds, kk)
                    dkk[par] = dkk[par] + _dot_tn(ds, q)
                    dvv[par] = dvv[par] + _dot_tn(p.astype(BF16), do_b)
                dq_ref[:, sl] = dq
            for par in range(2):
                kv = slice((2 * g + par) * BLOCK, (2 * g + par + 1) * BLOCK)
                dkp_ref[:, kv] = dkk[par][0:BLOCK]
                dkc_ref[:, kv] = dkk[par][BLOCK:2 * BLOCK]
                dvp_ref[:, kv] = dvv[par][0:BLOCK]
                dvc_ref[:, kv] = dvv[par][BLOCK:2 * BLOCK]
        ds_ref[...] += dsink

    part = pl.BlockSpec((BLOCK, SB_W), lambda i: (i, 0))
    row = pl.BlockSpec((BLOCK, SW_W), lambda i: (i, 0))
    pshape = jax.ShapeDtypeStruct((L, SB_W), F32)
    return pl.pallas_call(
        body, name="swa_bwd", grid=(L // BLOCK,), in_specs=_swa_specs(L) + [row],
        out_specs=[row, part, part, part, part, pl.BlockSpec((8, BLOCK), lambda i: (0, 0))],
        out_shape=[jax.ShapeDtypeStruct((L, SW_W), F32), pshape, pshape, pshape, pshape,
                   jax.ShapeDtypeStruct((8, BLOCK), F32)],
        compiler_params=_cparams("arbitrary"),
    )(swq, swk4, swk4, swv4, swv4, sinks, do_sw)


def _gate_specs(tm):
    return [pl.BlockSpec((tm, SW_W), lambda i: (i, 0)), pl.BlockSpec((tm, D_MODEL), lambda i: (i, 1)),
            pl.BlockSpec((tm, D_MODEL), lambda i: (i, 2)), pl.BlockSpec((tm, SB_W), lambda i: (i, 6))]


def _post_fwd(o_sb, o_sw, gate, h0, w_bsb, w_bsw, w_out):
    L = h0.shape[0]
    tm = _row_tile(L, 320)

    def body(osb, osw, swz, gsb, gsw, sbz, h0_ref, wb1, wb2, wo, h1, usb, usw, mrg, ysb, ysw):
        z1 = sbz[...]
        z2 = swz[...]
        u1 = (osb[...] * (z1 * jax.nn.sigmoid(z1))).astype(BF16)
        u2 = (osw[...] * (z2 * jax.nn.sigmoid(z2))).astype(BF16)
        y1 = _dot(u1, wb1[...])
        y2 = _dot(u2, wb2[...])
        merged = (jax.nn.sigmoid(gsb[...]) * y1 + jax.nn.sigmoid(gsw[...]) * y2).astype(BF16)
        h1[...] = h0_ref[...] + _dot(merged, wo[...])
        usb[...] = u1
        usw[...] = u2
        mrg[...] = merged
        ysb[...] = y1
        ysw[...] = y2

    row = lambda w: pl.BlockSpec((tm, w), lambda i: (i, 0))
    whole = lambda a: pl.BlockSpec(a.shape, lambda i: (0, 0))
    shp = lambda w, dt: jax.ShapeDtypeStruct((L, w), dt)
    return pl.pallas_call(
        body, name="post_fwd", grid=(L // tm,),
        in_specs=[row(SB_W), row(SW_W)] + _gate_specs(tm) + [row(D_MODEL), whole(w_bsb), whole(w_bsw), whole(w_out)],
        out_specs=[row(D_MODEL), row(SB_W), row(SW_W), row(D_MODEL), row(D_MODEL), row(D_MODEL)],
        out_shape=[shp(D_MODEL, F32), shp(SB_W, BF16), shp(SW_W, BF16), shp(D_MODEL, BF16),
                   shp(D_MODEL, F32), shp(D_MODEL, F32)],
        compiler_params=_cparams("parallel"),
    )(o_sb, o_sw, gate, gate, gate, gate, h0, w_bsb, w_bsw, w_out)


def _loss_bwd(h1, target, gain2):
    L = h1.shape[0]

    def body(h_ref, t_ref, g_ref, dh_ref, loss_ref, dg_ref):
        i = pl.program_id(0)

        @pl.when(i == 0)
        def _():
            dh_ref[...] = jnp.zeros_like(dh_ref)
            loss_ref[...] = jnp.zeros_like(loss_ref)
            dg_ref[...] = jnp.zeros_like(dg_ref)

        @pl.when(i > 0)
        def _():
            h = h_ref[...]
            g = g_ref[...]
            r = lax.rsqrt(jnp.mean(h * h, axis=-1, keepdims=True) + RMS_EPS)
            n = h * r
            err = n * g - t_ref[...]
            loss_ref[...] += 0.5 * jnp.sum(jnp.mean(err * err, axis=-1, keepdims=True))
            dy = err * (1.0 / D_MODEL)
            dg_ref[...] += jnp.sum(dy * n, axis=0, keepdims=True)
            dn = dy * g
            dh_ref[...] = r * (dn - n * jnp.mean(dn * n, axis=-1, keepdims=True))

    return pl.pallas_call(
        body, name="loss_bwd", grid=(L // BLOCK,),
        in_specs=[pl.BlockSpec((BLOCK, D_MODEL), lambda i: (i, 0)),
                  pl.BlockSpec((BLOCK, D_MODEL), lambda i: (jnp.maximum(i - 1, 0), 0)),
                  pl.BlockSpec((1, D_MODEL), lambda i: (0, 0))],
        out_specs=[pl.BlockSpec((BLOCK, D_MODEL), lambda i: (i, 0)), pl.BlockSpec((1, BLOCK), lambda i: (0, 0)),
                   pl.BlockSpec((1, D_MODEL), lambda i: (0, 0))],
        out_shape=[jax.ShapeDtypeStruct((L, D_MODEL), F32), jax.ShapeDtypeStruct((1, BLOCK), F32),
                   jax.ShapeDtypeStruct((1, D_MODEL), F32)],
        compiler_params=_cparams("arbitrary"),
    )(h1, target, gain2)


def _post_bwd(dh1, gate, y_sb, y_sw, o_sb, o_sw, wt_out, wt_bsb, wt_bsw):
    L = dh1.shape[0]
    tm = _row_tile(L, 320)

    def body(dh, swz, gsb, gsw, sbz, ysb, ysw, osb, osw, wo, wb1, wb2, dy1_ref, dy2_ref, do1_ref, do2_ref, dg_ref):
        dm = _dot(dh[...].astype(BF16), wo[...])
        s1 = jax.nn.sigmoid(gsb[...])
        s2 = jax.nn.sigmoid(gsw[...])
        dy1 = (dm * s1).astype(BF16)
        dy2 = (dm * s2).astype(BF16)
        dy1_ref[...] = dy1
        dy2_ref[...] = dy2
        du1 = _dot(dy1, wb1[...])
        du2 = _dot(dy2, wb2[...])
        z1 = sbz[...]
        z2 = swz[...]
        sz1 = jax.nn.sigmoid(z1)
        sz2 = jax.nn.sigmoid(z2)
        do1_ref[...] = du1 * (z1 * sz1)
        do2_ref[...] = du2 * (z2 * sz2)
        dg_ref[:, 0:SW_W] = (du2 * osw[...] * (sz2 * (1.0 + z2 * (1.0 - sz2)))).astype(BF16)
        dg_ref[:, SW_W:SW_W + D_MODEL] = (dm * ysb[...] * (s1 * (1.0 - s1))).astype(BF16)
        dg_ref[:, SW_W + D_MODEL:SW_W + 2 * D_MODEL] = (dm * ysw[...] * (s2 * (1.0 - s2))).astype(BF16)
        dg_ref[:, SW_W + 2 * D_MODEL:GATE_W] = (du1 * osb[...] * (sz1 * (1.0 + z1 * (1.0 - sz1)))).astype(BF16)

    row = lambda w: pl.BlockSpec((tm, w), lambda i: (i, 0))
    whole = lambda a: pl.BlockSpec(a.shape, lambda i: (0, 0))
    shp = lambda w, dt: jax.ShapeDtypeStruct((L, w), dt)
    return pl.pallas_call(
        body, name="post_bwd", grid=(L // tm,),
        in_specs=[row(D_MODEL)] + _gate_specs(tm) + [row(D_MODEL), row(D_MODEL), row(SB_W), row(SW_W),
                                                     whole(wt_out), whole(wt_bsb), whole(wt_bsw)],
        out_specs=[row(D_MODEL), row(D_MODEL), row(SB_W), row(SW_W), row(GATE_W)],
        out_shape=[shp(D_MODEL, BF16), shp(D_MODEL, BF16), shp(SB_W, F32), shp(SW_W, F32), shp(GATE_W, BF16)],
        compiler_params=_cparams("parallel"),
    )(dh1, gate, gate, gate, gate, y_sb, y_sw, o_sb, o_sw, wt_out, wt_bsb, wt_bsw)


def _assemble_dqkv(dswq, dsbq, dsbk, dsbv, dkp, dkc, dvp, dvc, cos_t, sin_t):
    L = dswq.shape[0]
    tm = BLOCK
    nb = L // BLOCK

    def body(dq_ref, dsq_ref, dsk_ref, dsv_ref, dkp_ref, dkc_ref, dvp_ref, dvc_ref, c_ref, s_ref, o_ref):
        C = c_ref[...]
        S = s_ref[...]
        lane = lax.broadcasted_iota(jnp.int32, (tm, BLOCK), 1)
        lo32 = lane % HEAD_DIM < HEAD_DIM // 2
        lo = lane < HEAD_DIM
        has_next = (pl.program_id(0) + 1 < nb).astype(F32)

        def unrope(dy):
            return dy * C + _rope_partner(dy * S, lo32)

        def fold(cur_ref, next_ref):
            b00, b01, b10, b11 = (cur_ref[:, k * BLOCK:(k + 1) * BLOCK] + has_next * next_ref[:, k * BLOCK:(k + 1) * BLOCK]
                                  for k in range(4))
            return jnp.where(lo, b00 + pltpu.roll(b01, HEAD_DIM, 1), pltpu.roll(b10, HEAD_DIM, 1) + b11)

        for m in range(SW_W // BLOCK):
            sl = slice(m * BLOCK, (m + 1) * BLOCK)
            o_ref[:, sl] = unrope(dq_ref[:, sl] * Q_SCALE).astype(BF16)
        o_ref[:, SW_W:SW_W + SB_W] = (dsq_ref[...] * Q_SCALE).astype(BF16)
        o_ref[:, SW_W + SB_W:SW_W + 2 * SB_W] = dsk_ref[...].astype(BF16)
        o_ref[:, SW_W + 2 * SB_W:SW_W + 3 * SB_W] = dsv_ref[...].astype(BF16)
        base = SW_W + 3 * SB_W
        o_ref[:, base:base + KV_W] = unrope(fold(dkc_ref, dkp_ref)).astype(BF16)
        o_ref[:, base + KV_W:base + 2 * KV_W] = fold(dvc_ref, dvp_ref).astype(BF16)

    row = lambda w: pl.BlockSpec((tm, w), lambda i: (i, 0))
    nxt = pl.BlockSpec((tm, SB_W), lambda i: (jnp.minimum(i + 1, nb - 1), 0))
    return pl.pallas_call(
        body, name="assemble_dqkv", grid=(nb,),
        in_specs=[row(SW_W), row(SB_W), row(SB_W), row(SB_W), nxt, row(SB_W), nxt, row(SB_W), row(BLOCK), row(BLOCK)],
        out_specs=row(QKV_W), out_shape=jax.ShapeDtypeStruct((L, QKV_W), BF16),
        compiler_params=_cparams("parallel"),
    )(dswq, dsbq, dsbk, dsbv, dkp, dkc, dvp, dvc, cos_t, sin_t)


def _norm_bwd(dxn, h0, dh1, gain):
    L = h0.shape[0]

    def body(dx_ref, h_ref, dh_ref, g_ref, gx_ref, dm_ref, dg_ref):
        i = pl.program_id(0)

        @pl.when(i == 0)
        def _():
            dg_ref[...] = jnp.zeros_like(dg_ref)

        h = h_ref[...]
        dxn_t = dx_ref[...]
        r = lax.rsqrt(jnp.mean(h * h, axis=-1, keepdims=True) + RMS_EPS)
        n = h * r
        dg_ref[...] += jnp.sum(dxn_t * n, axis=0, keepdims=True)
        dn = dxn_t * g_ref[...]
        dh0 = dh_ref[...] + r * (dn - n * jnp.mean(dn * n, axis=-1, keepdims=True))
        gx_ref[...] = dh0

        @pl.when(i == 0)
        def _():
            dm_ref[...] = dh0[PAD:BLOCK]

    row = pl.BlockSpec((BLOCK, D_MODEL), lambda i: (i, 0))
    return pl.pallas_call(
        body, name="norm_bwd", grid=(L // BLOCK,),
        in_specs=[row, row, row, pl.BlockSpec((1, D_MODEL), lambda i: (0, 0))],
        out_specs=[pl.BlockSpec((BLOCK, D_MODEL), lambda i: (jnp.maximum(i - 1, 0), 0)),
                   pl.BlockSpec((N_META, D_MODEL), lambda i: (0, 0)), pl.BlockSpec((1, D_MODEL), lambda i: (0, 0))],
        out_shape=[jax.ShapeDtypeStruct((L - BLOCK, D_MODEL), F32), jax.ShapeDtypeStruct((N_META, D_MODEL), F32),
                   jax.ShapeDtypeStruct((1, D_MODEL), F32)],
        compiler_params=_cparams("arbitrary"),
    )(dxn, h0, dh1, gain)


def _adamw(w, g, m, v, name):
    R, C = w.shape
    tr = _row_tile(R, 256) if R % 16 == 0 else R

    def body(w_ref, g_ref, m_ref, v_ref, d_ref, nm_ref, nv_ref):
        g = g_ref[...]
        m_new = ADAM_B1 * m_ref[...] + (1.0 - ADAM_B1) * g
        v_new = ADAM_B2 * v_ref[...] + (1.0 - ADAM_B2) * (g * g)
        m_hat = m_new / (1.0 - ADAM_B1 ** ADAM_STEP)
        v_hat = v_new / (1.0 - ADAM_B2 ** ADAM_STEP)
        d_ref[...] = -ADAM_LR * (m_hat / (jnp.sqrt(v_hat) + ADAM_EPS) + ADAM_WD * w_ref[...])
        nm_ref[...] = m_new
        nv_ref[...] = v_new

    spec = pl.BlockSpec((tr, C), lambda i: (i, 0))
    shp = jax.ShapeDtypeStruct((R, C), F32)
    return pl.pallas_call(
        body, name=name, grid=(R // tr,), in_specs=[spec] * 4, out_specs=[spec] * 3, out_shape=[shp] * 3,
        compiler_params=_cparams("parallel"),
    )(w, g, m, v)


def _sum_leading(a, name):
    n, R, C = a.shape
    tr = _row_tile(R, 160) if R % 16 == 0 else R

    def body(a_ref, o_ref):
        acc = a_ref[0]
        for k in range(1, n):
            acc = acc + a_ref[k]
        o_ref[...] = acc

    return pl.pallas_call(
        body, name=name, grid=(R // tr,), in_specs=[pl.BlockSpec((n, tr, C), lambda i: (0, i, 0))],
        out_specs=pl.BlockSpec((tr, C), lambda i: (i, 0)), out_shape=jax.ShapeDtypeStruct((R, C), a.dtype),
        compiler_params=_cparams("parallel"),
    )(a)


def _place():
    x, y, c = lax.axis_index("x"), lax.axis_index("y"), lax.axis_index("c")
    return x, y, c


def _all_gather(block, name):
    R, C = block.shape

    def body(x_ref, out_ref, send_sems, recv_sems, local_sem):
        x, y, c = _place()
        me, sibling = (x, y, c), (x, y, 1 - c)
        chips = [(1 - x, y), (x, 1 - y), (1 - x, 1 - y)]

        def slot(px, py, pc):
            return out_ref.at[4 * px + 2 * py + pc]

        def copy(k, blk, to, src=None):
            return pltpu.make_async_remote_copy(
                src_ref=slot(*blk) if src is None else src, dst_ref=slot(*blk),
                send_sem=send_sems.at[k], recv_sem=recv_sems.at[k], device_id=to, device_id_type=MESH)

        mine = pltpu.make_async_copy(x_ref, slot(*me), local_sem)
        mine.start()
        first = [copy(0, me, sibling, src=x_ref)]
        first += [copy(1 + j, me, (*chip, c), src=x_ref) for j, chip in enumerate(chips)]
        for cp in first:
            cp.start()
        passed = [copy(4 + j, (*chip, c), sibling) for j, chip in enumerate(chips)]
        for j, chip in enumerate(chips):
            copy(1 + j, (*chip, c), me).wait_recv()
            passed[j].start()
        copy(0, sibling, me).wait_recv()
        for j, chip in enumerate(chips):
            copy(4 + j, (*chip, 1 - c), me).wait_recv()
        for cp in first + passed:
            cp.wait_send()
        mine.wait()

    return pl.pallas_call(
        body, name=name, in_specs=[ANY], out_specs=ANY,
        out_shape=jax.ShapeDtypeStruct((N_DEV, R, C), block.dtype),
        scratch_shapes=[pltpu.SemaphoreType.DMA((7,)), pltpu.SemaphoreType.DMA((7,)), pltpu.SemaphoreType.DMA],
    )(block)


def _pair_exchange(g8):
    _, R, C = g8.shape

    def body(g_ref, out_ref, send_sems, recv_sems, local_sems):
        x, y, c = _place()
        sibling = (x, y, 1 - c)
        local = [pltpu.make_async_copy(g_ref.at[2 * s + c], out_ref.at[c, s], local_sems.at[s]) for s in range(N_CHIP)]
        remote = [pltpu.make_async_remote_copy(
            src_ref=g_ref.at[2 * s + (1 - c)], dst_ref=out_ref.at[c, s], send_sem=send_sems.at[s],
            recv_sem=recv_sems.at[s], device_id=sibling, device_id_type=MESH) for s in range(N_CHIP)]
        for cp in remote + local:
            cp.start()
        for s in range(N_CHIP):
            pltpu.make_async_remote_copy(
                src_ref=g_ref.at[s], dst_ref=out_ref.at[1 - c, s], send_sem=send_sems.at[s],
                recv_sem=recv_sems.at[s], device_id=sibling, device_id_type=MESH).wait_recv()
        for cp in remote:
            cp.wait_send()
        for cp in local:
            cp.wait()

    return pl.pallas_call(
        body, name="grad_pair_exchange", in_specs=[ANY], out_specs=ANY,
        out_shape=jax.ShapeDtypeStruct((2, N_CHIP, R, C), g8.dtype),
        scratch_shapes=[pltpu.SemaphoreType.DMA((N_CHIP,)), pltpu.SemaphoreType.DMA((N_CHIP,)),
                        pltpu.SemaphoreType.DMA((N_CHIP,))],
    )(g8)


def _chip_scatter(p4):
    _, R, C = p4.shape

    def body(p_ref, out_ref, send_sems, recv_sems, local_sem):
        x, y, c = _place()
        my_chip = 2 * x + y
        chips = [(1 - x, y), (x, 1 - y), (1 - x, 1 - y)]
        local = pltpu.make_async_copy(p_ref.at[my_chip], out_ref.at[my_chip], local_sem)
        local.start()
        sends = [pltpu.make_async_remote_copy(
            src_ref=p_ref.at[2 * cx + cy], dst_ref=out_ref.at[my_chip], send_sem=send_sems.at[j],
            recv_sem=recv_sems.at[j], device_id=(cx, cy, c), device_id_type=MESH) for j, (cx, cy) in enumerate(chips)]
        for cp in sends:
            cp.start()
        for j, (cx, cy) in enumerate(chips):
            pltpu.make_async_remote_copy(
                src_ref=p_ref.at[my_chip], dst_ref=out_ref.at[2 * cx + cy], send_sem=send_sems.at[j],
                recv_sem=recv_sems.at[j], device_id=(cx, cy, c), device_id_type=MESH).wait_recv()
        for cp in sends:
            cp.wait_send()
        local.wait()

    return pl.pallas_call(
        body, name="grad_chip_scatter", in_specs=[ANY], out_specs=ANY,
        out_shape=jax.ShapeDtypeStruct((N_CHIP, R, C), p4.dtype),
        scratch_shapes=[pltpu.SemaphoreType.DMA((3,)), pltpu.SemaphoreType.DMA((3,)), pltpu.SemaphoreType.DMA],
    )(p4)


def _pair_share(r):
    R, C = r.shape

    def body(r_ref, out_ref, send_sem, recv_sem, local_sem):
        x, y, c = _place()
        sibling = (x, y, 1 - c)
        local = pltpu.make_async_copy(r_ref, out_ref.at[c], local_sem)
        local.start()
        send = pltpu.make_async_remote_copy(src_ref=r_ref, dst_ref=out_ref.at[c], send_sem=send_sem,
                                            recv_sem=recv_sem, device_id=sibling, device_id_type=MESH)
        send.start()
        pltpu.make_async_remote_copy(src_ref=r_ref, dst_ref=out_ref.at[1 - c], send_sem=send_sem,
                                     recv_sem=recv_sem, device_id=sibling, device_id_type=MESH).wait_recv()
        send.wait_send()
        local.wait()

    return pl.pallas_call(
        body, name="grad_pair_share", in_specs=[ANY], out_specs=ANY,
        out_shape=jax.ShapeDtypeStruct((2, R, C), r.dtype),
        scratch_shapes=[pltpu.SemaphoreType.DMA, pltpu.SemaphoreType.DMA, pltpu.SemaphoreType.DMA],
    )(r)


def _gather_direct(block, name):
    R, C = block.shape

    def body(x_ref, out_ref, send_sems, recv_sems, local_sem):
        x, y, c = _place()
        me = 4 * x + 2 * y + c
        local = pltpu.make_async_copy(x_ref, out_ref.at[me], local_sem)
        local.start()
        peers = []
        for k in range(1, N_DEV):
            px, py, pc = x ^ (k >> 2), y ^ ((k >> 1) & 1), c ^ (k & 1)
            peers.append((k, (px, py, pc), 4 * px + 2 * py + pc))
        sends = [pltpu.make_async_remote_copy(
            src_ref=x_ref, dst_ref=out_ref.at[me], send_sem=send_sems.at[k - 1], recv_sem=recv_sems.at[k - 1],
            device_id=dev, device_id_type=MESH) for k, dev, _ in peers]
        for cp in sends:
            cp.start()
        for k, dev, idx in peers:
            pltpu.make_async_remote_copy(
                src_ref=x_ref, dst_ref=out_ref.at[idx], send_sem=send_sems.at[k - 1], recv_sem=recv_sems.at[k - 1],
                device_id=dev, device_id_type=MESH).wait_recv()
        for cp in sends:
            cp.wait_send()
        local.wait()

    return pl.pallas_call(
        body, name=name, in_specs=[ANY], out_specs=ANY,
        out_shape=jax.ShapeDtypeStruct((N_DEV, R, C), block.dtype),
        scratch_shapes=[pltpu.SemaphoreType.DMA((7,)), pltpu.SemaphoreType.DMA((7,)), pltpu.SemaphoreType.DMA],
    )(block)


W_IN_SHARD = IN_COLS // N_CHIP
ROWS_W_IN = (D_MODEL // 2) * W_IN_SHARD // D_MODEL
ROWS_BSB = (SB_W // 2) * (D_MODEL // N_CHIP) // D_MODEL
ROWS_SQ = D_MODEL // N_CHIP // 2
ROWS_BIG = ROWS_W_IN + ROWS_BSB + 2 * ROWS_SQ


def _half(a, c, axis):
    n = a.shape[axis] // 2
    return lax.dynamic_slice_in_dim(a, c * n, n, axis)


def _pack_shard_half(w_in, w_bsb, w_bsw, w_out, c):
    parts = [_half(w_in, c, 0).reshape(ROWS_W_IN, D_MODEL), _half(w_bsb, c, 0).reshape(ROWS_BSB, D_MODEL),
             _half(w_bsw, c, 0), _half(w_out, c, 0)]
    return jnp.concatenate(parts, axis=0)


def _unpack_full(blocks):
    b = blocks.reshape(N_CHIP, 2, ROWS_BIG, D_MODEL)
    o = 0
    w_in = b[:, :, o:o + ROWS_W_IN].reshape(N_CHIP, D_MODEL, W_IN_SHARD)
    o += ROWS_W_IN
    w_bsb = b[:, :, o:o + ROWS_BSB].reshape(N_CHIP, SB_W, D_MODEL // N_CHIP)
    o += ROWS_BSB
    w_bsw = b[:, :, o:o + ROWS_SQ].reshape(D_MODEL, D_MODEL)
    o += ROWS_SQ
    w_out = b[:, :, o:o + ROWS_SQ].reshape(D_MODEL, D_MODEL)
    w_in = jnp.transpose(w_in, (1, 0, 2)).reshape(D_MODEL, IN_COLS)
    w_bsb = jnp.transpose(w_bsb, (1, 0, 2)).reshape(SB_W, D_MODEL)
    return w_in, w_bsb, w_bsw, w_out


def _pack_grads(dw_in, dw_bsb, dw_bsw, dw_out):
    a = jnp.transpose(dw_in.reshape(2, D_MODEL // 2, N_CHIP, W_IN_SHARD), (2, 0, 1, 3)).reshape(N_CHIP, 2, ROWS_W_IN, D_MODEL)
    b = jnp.transpose(dw_bsb.reshape(2, SB_W // 2, N_CHIP, D_MODEL // N_CHIP), (2, 0, 1, 3)).reshape(N_CHIP, 2, ROWS_BSB, D_MODEL)
    c = dw_bsw.reshape(N_CHIP, 2, ROWS_SQ, D_MODEL)
    d = dw_out.reshape(N_CHIP, 2, ROWS_SQ, D_MODEL)
    return jnp.concatenate([a, b, c, d], axis=2).reshape(N_DEV, ROWS_BIG, D_MODEL)


def _unpack_shard(full2):
    o = 0
    w_in = full2[:, o:o + ROWS_W_IN].reshape(D_MODEL, W_IN_SHARD)
    o += ROWS_W_IN
    w_bsb = full2[:, o:o + ROWS_BSB].reshape(SB_W, D_MODEL // N_CHIP)
    o += ROWS_BSB
    w_bsw = full2[:, o:o + ROWS_SQ].reshape(D_MODEL // N_CHIP, D_MODEL)
    o += ROWS_SQ
    w_out = full2[:, o:o + ROWS_SQ].reshape(D_MODEL // N_CHIP, D_MODEL)
    return w_in, w_bsb, w_bsw, w_out


_QKV_FROM_IN = ((1536, 2560), (0, 512), (512, 1024), (1024, 1536), (2560, 2688), (2688, 2816))
_GATE_FROM_IN = ((3328, 4352), (4352, 5376), (5376, 6400), (2816, 3328))


def _split_w_in(w_in):
    qkv = jnp.concatenate([w_in[:, a:b] for a, b in _QKV_FROM_IN], axis=1)
    gate = jnp.concatenate([w_in[:, a:b] for a, b in _GATE_FROM_IN], axis=1)
    return qkv, gate


def _join_dw_in(dqkv, dgate):
    q = lambda a, b: dqkv[:, a:b]
    g = lambda a, b: dgate[:, a:b]
    return jnp.concatenate([q(1024, 1536), q(1536, 2048), q(2048, 2560), q(0, 1024), q(2560, 2688), q(2688, 2816),
                            g(3072, 3584), g(0, 1024), g(1024, 2048), g(2048, 3072)], axis=1)


def _rope_tables(L):
    half = HEAD_DIM // 2
    inv = ROPE_THETA ** (-jnp.arange(half, dtype=F32) / half)
    pos = (jnp.arange(L) - PAD).astype(F32)
    ang = pos[:, None] * inv[None, :]
    cos, sin = jnp.cos(ang), jnp.sin(ang)
    return jnp.tile(jnp.concatenate([cos, cos], axis=1), (1, 2)), jnp.tile(jnp.concatenate([-sin, sin], axis=1), (1, 2))


def kernel(x, meta_tokens, norm_gain, w_in, w_branch_sb, w_branch_swa, w_out, attn_sinks, final_norm_gain, loss_target, m_meta_tokens, m_norm_gain, m_w_in, m_w_branch_sb, m_w_branch_swa, m_w_out, m_attn_sinks, m_final_norm_gain, v_meta_tokens, v_norm_gain, v_w_in, v_w_branch_sb, v_w_branch_swa, v_w_out, v_attn_sinks, v_final_norm_gain):
    xi, yi, ci = _place()
    chip = 2 * xi + yi
    seq = x.shape[1]
    L = seq + BLOCK

    wpack = _pack_shard_half(w_in[0], w_branch_sb[0], w_branch_swa[0], w_out[0], ci).astype(BF16)
    w_in_f, w_bsb_f, w_bsw_f, w_out_f = _unpack_full(_all_gather(wpack, "weight_all_gather"))
    mg = _all_gather(_half(meta_tokens, ci, 0), "meta_all_gather").reshape(N_CHIP, 2, N_META // 2, D_MODEL // N_CHIP)
    meta_full = jnp.transpose(mg, (1, 2, 0, 3)).reshape(N_META, D_MODEL)
    w_qkv, w_gate = _split_w_in(w_in_f)

    h0 = jnp.concatenate([jnp.zeros((PAD, D_MODEL), F32), meta_full, x[0]], axis=0)
    cos_t, sin_t = _rope_tables(L)
    xn = _norm_fwd(h0, norm_gain)
    qkv = _mm([(xn, w_qkv)], F32, "in_proj_qkv", tn_pref=1408)
    gate = _mm([(xn, w_gate)], F32, "in_proj_gate")
    swq, sbq, klo, khi, vlo, vhi, swk4, swv4 = _prep_qkv(qkv, cos_t, sin_t)
    o_sb = _sb_fwd(sbq, klo, khi, vlo, vhi)
    o_sw = _swa_fwd(swq, swk4, swv4, attn_sinks)
    h1, u_sb, u_sw, merged, y_sb, y_sw = _post_fwd(o_sb, o_sw, gate, h0, w_bsb_f, w_bsw_f, w_out_f)

    dh1, loss_part, dgain2 = _loss_bwd(h1, loss_target[0], final_norm_gain.reshape(1, D_MODEL))
    dy_sb, dy_sw, do_sb, do_sw, dgate = _post_bwd(dh1, gate, y_sb, y_sw, o_sb, o_sw, w_out_f.T, w_bsb_f.T, w_bsw_f.T)
    dsbq, dsbk, dsbv = _sb_bwd(sbq, klo, khi, vlo, vhi, do_sb)
    dswq, dkp, dkc, dvp, dvc, dsink = _swa_bwd(swq, swk4, swv4, attn_sinks, do_sw)
    dqkv = _assemble_dqkv(dswq, dsbq, dsbk, dsbv, dkp, dkc, dvp, dvc, cos_t, sin_t)
    dxn = _mm([(dqkv, w_qkv.T), (dgate, w_gate.T)], F32, "dxn", tm_pref=320, tn_pref=1024)
    grad_x, dmeta, dgain1 = _norm_bwd(dxn, h0, dh1, norm_gain)
    dw_qkv = _mm_tn(xn, dqkv, "dw_qkv", tn_pref=1408)
    dw_gate = _mm_tn(xn, dgate, "dw_gate")
    dw_bsb = _mm_tn(u_sb, dy_sb, "dw_branch_sb", tn_pref=1024)
    dw_bsw = _mm_tn(u_sw, dy_sw, "dw_branch_swa", tn_pref=1024)
    dw_out = _mm_tn(merged, dh1, "dw_out", tn_pref=1024)

    g8 = _pack_grads(_join_dw_in(dw_qkv, dw_gate), dw_bsb, dw_bsw, dw_out)
    pair = _pair_exchange(g8)
    p4 = _sum_leading(pair.reshape(2, N_CHIP * ROWS_BIG, D_MODEL), "grad_pair_sum").reshape(N_CHIP, ROWS_BIG, D_MODEL)
    r_half = _sum_leading(_chip_scatter(p4), "grad_chip_sum")
    g_in, g_bsb, g_bsw, g_out = _unpack_shard(_pair_share(r_half))

    small = jnp.concatenate([dmeta, dgain1, dgain2, jnp.pad(dsink[0:1], ((0, 0), (0, D_MODEL - BLOCK))),
                             jnp.pad(loss_part, ((0, 0), (0, D_MODEL - BLOCK))),
                             jnp.zeros((32 - N_META - 4, D_MODEL), F32)], axis=0)
    tot = _sum_leading(_gather_direct(small, "small_all_gather"), "small_sum")
    g_meta = lax.dynamic_slice_in_dim(tot[0:N_META], chip * (D_MODEL // N_CHIP), D_MODEL // N_CHIP, 1)
    g_gain1 = tot[N_META:N_META + 1]
    g_gain2 = tot[N_META + 1]
    g_sinks = tot[N_META + 2:N_META + 3, 0:16]
    loss = tot[N_META + 3, 0]

    def upd(w, g, m, v, name):
        shape = w.shape
        as2d = lambda a: a.reshape(-1, shape[-1])
        d, nm, nv = _adamw(as2d(w), as2d(g), as2d(m), as2d(v), name)
        return g.reshape(shape), d.reshape(shape), nm.reshape(shape), nv.reshape(shape)

    res = [
        upd(meta_tokens, g_meta, m_meta_tokens, v_meta_tokens, "adamw_meta"),
        upd(norm_gain, g_gain1, m_norm_gain, v_norm_gain, "adamw_norm_gain"),
        upd(w_in, g_in, m_w_in, v_w_in, "adamw_w_in"),
        upd(w_branch_sb, g_bsb, m_w_branch_sb, v_w_branch_sb, "adamw_w_branch_sb"),
        upd(w_branch_swa, g_bsw, m_w_branch_swa, v_w_branch_swa, "adamw_w_branch_swa"),
        upd(w_out, g_out, m_w_out, v_w_out, "adamw_w_out"),
        upd(attn_sinks, g_sinks, m_attn_sinks, v_attn_sinks, "adamw_attn_sinks"),
        upd(final_norm_gain, g_gain2, m_final_norm_gain, v_final_norm_gain, "adamw_final_norm_gain"),
    ]
    grads, deltas, new_m, new_v = zip(*res)
    return (loss, grad_x.reshape(1, seq, D_MODEL), *grads, *deltas, *new_m, *new_v)
```

```python
import functools

import jax
import jax.numpy as jnp
from jax import lax
from jax.experimental import pallas as pl
from jax.experimental.pallas import tpu as pltpu

F32 = jnp.float32
BF16 = jnp.bfloat16

D_MODEL = 1024
BLOCK = 128
N_META = 16
PAD = BLOCK - N_META
HEAD_DIM = 64
SB_W = 512
SW_W = 1024
KV_W = 128
QKV_W = SW_W + 3 * SB_W + 2 * KV_W
GATE_W = 2 * D_MODEL + SW_W + SB_W
IN_COLS = QKV_W + GATE_W
ROPE_THETA = 10000.0
RMS_EPS = 1e-6
Q_SCALE = HEAD_DIM ** -0.5

ADAM_LR = 0.001
ADAM_B1 = 0.9
ADAM_B2 = 0.999
ADAM_EPS = 1e-08
ADAM_WD = 0.01
ADAM_STEP = 10

N_DEV = 8
N_CHIP = 4
MESH = pl.DeviceIdType.MESH
VMEM_LIMIT = 56 * 1024 * 1024
ANY = pl.BlockSpec(memory_space=pl.ANY)


def _cparams(*sem):
    return pltpu.CompilerParams(dimension_semantics=sem, vmem_limit_bytes=VMEM_LIMIT)


def _row_tile(n, pref):
    best = None
    for t in range(16, pref + 1, 16):
        if n % t == 0:
            best = t
    assert best is not None, (n, pref)
    return best


def _col_tile(n, pref):
    best = None
    for t in range(128, pref + 1, 128):
        if n % t == 0:
            best = t
    assert best is not None, (n, pref)
    return best


def _dot(a, b):
    return jnp.dot(a, b, preferred_element_type=F32)


def _dot_nt(a, b):
    return lax.dot_general(a, b, (((1,), (1,)), ((), ())), preferred_element_type=F32)


def _dot_tn(a, b):
    return lax.dot_general(a, b, (((0,), (0,)), ((), ())), preferred_element_type=F32)


def _lane_lo(shape):
    return lax.broadcasted_iota(jnp.int32, shape, len(shape) - 1) % BLOCK < HEAD_DIM


def _mm(pairs, out_dtype, name, tm_pref=640, tn_pref=1792):
    M = pairs[0][0].shape[0]
    N = pairs[0][1].shape[1]
    tm = _row_tile(M, tm_pref)
    tn = _col_tile(N, tn_pref)
    n_pairs = len(pairs)

    def body(*refs):
        o_ref = refs[-1]
        acc = None
        for p in range(n_pairs):
            d = _dot(refs[2 * p][...].astype(BF16), refs[2 * p + 1][...])
            acc = d if acc is None else acc + d
        o_ref[...] = acc.astype(out_dtype)

    in_specs, args = [], []
    for a, b in pairs:
        k = a.shape[1]
        in_specs += [pl.BlockSpec((tm, k), lambda n, m: (m, 0)), pl.BlockSpec((k, tn), lambda n, m: (0, n))]
        args += [a, b]
    return pl.pallas_call(
        body, name=name, grid=(N // tn, M // tm), in_specs=in_specs,
        out_specs=pl.BlockSpec((tm, tn), lambda n, m: (m, n)),
        out_shape=jax.ShapeDtypeStruct((M, N), out_dtype),
        compiler_params=_cparams("parallel", "arbitrary"),
    )(*args)


def _mm_tn(a, b, name, tn_pref=1792, tl_pref=640):
    L, M = a.shape
    N = b.shape[1]
    tn = _col_tile(N, tn_pref)
    tl = _row_tile(L, tl_pref)

    def body(a_ref, b_ref, o_ref):
        @pl.when(pl.program_id(1) == 0)
        def _():
            o_ref[...] = jnp.zeros_like(o_ref)
        o_ref[...] += _dot_tn(a_ref[...].astype(BF16), b_ref[...].astype(BF16))

    return pl.pallas_call(
        body, name=name, grid=(N // tn, L // tl),
        in_specs=[pl.BlockSpec((tl, M), lambda n, l: (l, 0)), pl.BlockSpec((tl, tn), lambda n, l: (l, n))],
        out_specs=pl.BlockSpec((M, tn), lambda n, l: (0, n)),
        out_shape=jax.ShapeDtypeStruct((M, N), F32),
        compiler_params=_cparams("parallel", "arbitrary"),
    )(a, b)


def _norm_fwd(h0, gain):
    L = h0.shape[0]
    tm = _row_tile(L, 640)

    def body(h_ref, g_ref, o_ref):
        h = h_ref[...]
        r = lax.rsqrt(jnp.mean(h * h, axis=-1, keepdims=True) + RMS_EPS)
        o_ref[...] = ((h * r) * g_ref[...]).astype(BF16)

    return pl.pallas_call(
        body, name="norm_fwd", grid=(L // tm,),
        in_specs=[pl.BlockSpec((tm, D_MODEL), lambda i: (i, 0)), pl.BlockSpec((1, D_MODEL), lambda i: (0, 0))],
        out_specs=pl.BlockSpec((tm, D_MODEL), lambda i: (i, 0)),
        out_shape=jax.ShapeDtypeStruct((L, D_MODEL), BF16),
        compiler_params=_cparams("parallel"),
    )(h0, gain)


def _rope_partner(x, lo32):
    return jnp.where(lo32, pltpu.roll(x, 96, 1), pltpu.roll(x, 32, 1))


def _prep_qkv(qkv, cos_t, sin_t):
    L = qkv.shape[0]
    tm = BLOCK * max(d for d in range(1, 6) if (L // BLOCK) % d == 0)
    n_steps = L // tm
    assert L // BLOCK + tm // BLOCK >= _padded_blocks(L)

    def body(x_ref, c_ref, s_ref, swq, sbq, kcat, vcat, swk4, swv4):
        C = c_ref[...]
        S = s_ref[...]
        lane = lax.broadcasted_iota(jnp.int32, (tm, BLOCK), 1)
        lo32 = lane % HEAD_DIM < HEAD_DIM // 2
        lo = lane < HEAD_DIM

        def rope(x):
            return x * C + _rope_partner(x, lo32) * S

        for m in range(SW_W // BLOCK):
            sl = slice(m * BLOCK, (m + 1) * BLOCK)
            swq[:, sl] = (rope(x_ref[:, sl]) * Q_SCALE).astype(BF16)
        for m in range(SB_W // BLOCK):
            sl = slice(m * BLOCK, (m + 1) * BLOCK)
            sbq[:, sl] = (x_ref[:, SW_W + m * BLOCK:SW_W + (m + 1) * BLOCK] * Q_SCALE).astype(BF16)
            k = x_ref[:, SW_W + SB_W + m * BLOCK:SW_W + SB_W + (m + 1) * BLOCK]
            v = x_ref[:, SW_W + 2 * SB_W + m * BLOCK:SW_W + 2 * SB_W + (m + 1) * BLOCK]
            for src, dst in ((k, kcat), (v, vcat)):
                even = jnp.where(lo, src, 0.0).astype(BF16)
                odd = jnp.where(lo, 0.0, src).astype(BF16)
                for b in range(tm // BLOCK):
                    dst[2 * b * BLOCK:(2 * b + 1) * BLOCK, sl] = even[b * BLOCK:(b + 1) * BLOCK]
                    dst[(2 * b + 1) * BLOCK:(2 * b + 2) * BLOCK, sl] = odd[b * BLOCK:(b + 1) * BLOCK]
        base = SW_W + 3 * SB_W
        kx = rope(x_ref[:, base:base + KV_W])
        vx = x_ref[:, base + KV_W:base + 2 * KV_W]
        for src, dst in ((kx, swk4), (vx, swv4)):
            sw = pltpu.roll(src, HEAD_DIM, 1)
            dst[:, 0:128] = jnp.where(lo, src, 0.0).astype(BF16)
            dst[:, 128:256] = jnp.where(lo, 0.0, sw).astype(BF16)
            dst[:, 256:384] = jnp.where(lo, sw, 0.0).astype(BF16)
            dst[:, 384:512] = jnp.where(lo, 0.0, src).astype(BF16)

        @pl.when(pl.program_id(0) == n_steps)
        def _():
            kcat[...] = jnp.zeros_like(kcat)
            vcat[...] = jnp.zeros_like(vcat)

    row = lambda w: pl.BlockSpec((tm, w), lambda i: (jnp.minimum(i, n_steps - 1), 0))
    row2 = pl.BlockSpec((2 * tm, SB_W), lambda i: (i, 0))
    shp = lambda r, w: jax.ShapeDtypeStruct((r, w), BF16)
    return pl.pallas_call(
        body, name="prep_qkv", grid=(n_steps + 1,),
        in_specs=[row(QKV_W), row(BLOCK), row(BLOCK)],
        out_specs=[row(SW_W), row(SB_W), row2, row2, row(SB_W), row(SB_W)],
        out_shape=[shp(L, SW_W), shp(L, SB_W), shp(2 * (L + tm), SB_W), shp(2 * (L + tm), SB_W), shp(L, SB_W),
                   shp(L, SB_W)],
        compiler_params=_cparams("arbitrary"),
    )(qkv, cos_t, sin_t)


WIDE = 2 * BLOCK
GROUP = 4
SUPER = GROUP * WIDE


def _padded_blocks(L):
    return -(-(L // BLOCK) // GROUP) * GROUP


def _cumsum_matrices():
    r = lax.broadcasted_iota(jnp.int32, (WIDE, WIDE), 0)
    c = lax.broadcasted_iota(jnp.int32, (WIDE, WIDE), 1)
    same = (r < BLOCK) == (c < BLOCK)
    rev = jnp.where(same & (r >= c), 1.0, 0.0).astype(BF16)
    fwd = jnp.where(same & (r <= c), 1.0, 0.0).astype(BF16)
    return rev, fwd


def _cumsum_hilo(x, U):
    hi = x.astype(BF16)
    lo = (x - hi.astype(F32)).astype(BF16)
    return _dot(hi, U) + _dot(lo, U)


def _head_totals(c, col):
    half = lax.broadcasted_iota(jnp.int32, c.shape, 1) < BLOCK
    return jnp.where(half, c[:, col:col + 1], c[:, BLOCK + col:BLOCK + col + 1])


def _chunks(x):
    return [x[:, t * WIDE:(t + 1) * WIDE] for t in range(GROUP)]


def _sb_diag_mask(i):
    t = i * BLOCK + lax.broadcasted_iota(jnp.int32, (BLOCK, SUPER), 0)
    lane = lax.broadcasted_iota(jnp.int32, (BLOCK, SUPER), 1)
    s = ((i // GROUP) * GROUP + lane // WIDE) * BLOCK + lane % BLOCK
    return (s < t) & (s >= PAD)


def _sb_scores(q, kc, U_rev, mask):
    z = _dot_nt(q, kc)
    sp = jnp.log(1.0 + jnp.exp(-jnp.abs(z)))
    lb = jnp.minimum(z, 0.0) - sp
    l1m = lb - z
    if mask is not None:
        l1m = jnp.where(mask, l1m, 0.0)
    cs = [_cumsum_hilo(x, U_rev) for x in _chunks(l1m)]
    ys, run = [None] * GROUP, None
    for t in reversed(range(GROUP)):
        y = z[:, t * WIDE:(t + 1) * WIDE] + cs[t]
        ys[t] = y if run is None else y + run
        tot = _head_totals(cs[t], 0)
        run = tot if run is None else run + tot
    return jnp.concatenate(ys, axis=1), run, lb


def _super_rows(J):
    return pl.ds(pl.multiple_of(J * SUPER, SUPER), SUPER)


def _sb_fwd(sbq, kcat, vcat):
    L = sbq.shape[0]
    nq = L // BLOCK

    def body(q_ref, k_ref, v_ref, o_ref):
        i = pl.program_id(1)
        jd = i // GROUP
        q = q_ref[...]
        U_rev, _ = _cumsum_matrices()

        def scores(J, mask=None):
            y, tot, _ = _sb_scores(q, k_ref[_super_rows(J), :], U_rev, mask)
            return y, tot

        def weighted(J, y, later, mask=None):
            w = jnp.exp(y + jnp.concatenate([later] * GROUP, axis=1))
            if mask is not None:
                w = jnp.where(mask, w, 0.0)
            return _dot(w.astype(BF16), v_ref[_super_rows(J), :])

        mask = _sb_diag_mask(i)
        y, tot = scores(jd, mask)
        acc = weighted(jd, y, jnp.zeros((BLOCK, WIDE), F32), mask)
        later = tot
        y, tot = scores(jnp.maximum(jd - 1, 0))

        def step(jj, st):
            acc, later, y, tot = st
            J = jd - 1 - jj
            acc = acc + weighted(J, y, later)
            y_next, tot_next = scores(J - 1)
            return acc, later + tot, y_next, tot_next

        acc, later, y, tot = lax.fori_loop(0, jnp.maximum(jd - 1, 0), step, (acc, later, y, tot))
        last = weighted(0, y, later)
        o_ref[...] = acc + jnp.where(jd >= 1, last, 0.0)

    qspec = pl.BlockSpec((BLOCK, BLOCK), lambda p, i: (i, p))
    kvspec = pl.BlockSpec((kcat.shape[0], BLOCK), lambda p, i: (0, p))
    return pl.pallas_call(
        body, name="sb_fwd", grid=(SB_W // BLOCK, nq),
        in_specs=[qspec, kvspec, kvspec], out_specs=qspec,
        out_shape=jax.ShapeDtypeStruct((L, SB_W), F32),
        compiler_params=_cparams("parallel", "arbitrary"),
    )(sbq, kcat, vcat)


def _sb_bwd(sbq, kcat, vcat, do_sb):
    L = sbq.shape[0]
    nq = L // BLOCK
    n_super = _padded_blocks(L) // GROUP
    LP = _padded_blocks(L) * BLOCK

    def body(q_ref, k_ref, v_ref, do_ref, dq_ref, dk_ref, dv_ref, g_scr, beta_scr):
        i = pl.program_id(1)
        jd = i // GROUP

        @pl.when(i == 0)
        def _():
            dk_ref[...] = jnp.zeros_like(dk_ref)
            dv_ref[...] = jnp.zeros_like(dv_ref)

        lo = _lane_lo((BLOCK, BLOCK))
        q = q_ref[...]
        qf = q.astype(F32)
        do = do_ref[...]
        do_b = do.astype(BF16)
        q_stack = jnp.concatenate([jnp.where(lo, qf, 0.0), jnp.where(lo, 0.0, qf)], axis=0).astype(BF16)
        do_stack = jnp.concatenate([jnp.where(lo, do, 0.0), jnp.where(lo, 0.0, do)], axis=0).astype(BF16)
        U_rev, U_fwd = _cumsum_matrices()
        mask = _sb_diag_mask(i)

        def stack(x):
            return jnp.concatenate([x[:, :BLOCK], x[:, BLOCK:]], axis=0)

        def add_key_rows(ref, J, x, other):
            for t, xt in enumerate(_chunks(x)):
                rows = pl.ds(pl.multiple_of((J * GROUP + t) * BLOCK, BLOCK), BLOCK)
                ref[rows, :] += _dot_tn(stack(xt), other)

        def scores(J, mask=None):
            y, tot, lb = _sb_scores(q, k_ref[_super_rows(J), :], U_rev, mask)
            beta_scr[J] = jnp.exp(lb).astype(BF16)
            return y, tot

        def weights(J, y, later, mask=None):
            w = jnp.exp(y + jnp.concatenate([later] * GROUP, axis=1))
            if mask is not None:
                w = jnp.where(mask, w, 0.0)
            g_scr[J] = _dot_nt(do_b, v_ref[_super_rows(J), :]) * w
            add_key_rows(dv_ref, J, w.astype(BF16), do_stack)

        y, tot = scores(jd, mask)
        weights(jd, y, jnp.zeros((BLOCK, WIDE), F32), mask)
        later = tot
        y, tot = scores(jnp.maximum(jd - 1, 0))

        def down(jj, st):
            later, y, tot = st
            J = jd - 1 - jj
            weights(J, y, later)
            y_next, tot_next = scores(J - 1)
            return later + tot, y_next, tot_next

        later, y, tot = lax.fori_loop(0, jnp.maximum(jd - 1, 0), down, (later, y, tot))

        @pl.when(jd >= 1)
        def _():
            weights(0, y, later)

        def prefix(J):
            excl, run = [], None
            for g in _chunks(g_scr[J]):
                upto = _dot(g.astype(BF16), U_fwd)
                e = upto - g
                excl.append(e if run is None else e + run)
                tot = _head_totals(upto, BLOCK - 1)
                run = tot if run is None else run + tot
            return jnp.concatenate(excl, axis=1), run

        def grads(J, excl, earlier, mask=None):
            g = g_scr[J]
            beta = beta_scr[J].astype(F32)
            dz = g * (1.0 - beta) - (excl + jnp.concatenate([earlier] * GROUP, axis=1)) * beta
            if mask is not None:
                dz = jnp.where(mask, dz, 0.0)
            dz = dz.astype(BF16)
            add_key_rows(dk_ref, J, dz, q_stack)
            return _dot(dz, k_ref[_super_rows(J), :])

        excl, tot_g = prefix(0)

        def up(J, st):
            dq, earlier, excl, tot_g = st
            dq = dq + grads(J, excl, earlier)
            excl_next, tot_next = prefix(J + 1)
            return dq, earlier + tot_g, excl_next, tot_next

        zero = jnp.zeros((BLOCK, WIDE), F32)
        dq, earlier, excl, tot_g = lax.fori_loop(0, jd, up, (jnp.zeros((BLOCK, BLOCK), F32), zero, excl, tot_g))
        dq_ref[...] = dq + grads(jd, excl, earlier, mask)

    qspec = pl.BlockSpec((BLOCK, BLOCK), lambda p, i: (i, p))
    catspec = pl.BlockSpec((kcat.shape[0], BLOCK), lambda p, i: (0, p))
    accspec = pl.BlockSpec((LP, BLOCK), lambda p, i: (0, p))
    acc_shape = jax.ShapeDtypeStruct((LP, SB_W), F32)
    return pl.pallas_call(
        body, name="sb_bwd", grid=(SB_W // BLOCK, nq),
        in_specs=[qspec, catspec, catspec, qspec],
        out_specs=[qspec, accspec, accspec], out_shape=[jax.ShapeDtypeStruct((L, SB_W), F32), acc_shape, acc_shape],
        scratch_shapes=[pltpu.VMEM((n_super, BLOCK, SUPER), F32), pltpu.VMEM((n_super, BLOCK, SUPER), BF16)],
        compiler_params=_cparams("parallel", "arbitrary"),
    )(sbq, kcat, vcat, do_sb)


def _swa_mask(i):
    t = lax.broadcasted_iota(jnp.int32, (BLOCK, 2 * BLOCK), 0)
    s = lax.broadcasted_iota(jnp.int32, (BLOCK, 2 * BLOCK), 1)
    diff = BLOCK + t - s
    return (diff >= 0) & (diff < BLOCK) & ((i - 1) * BLOCK + s >= PAD)


def _swa_probs(q, kk, mask, sink):
    sc = jnp.where(mask, _dot_nt(q, kk), -jnp.inf)
    mx = jnp.maximum(jnp.max(sc, axis=1, keepdims=True), sink)
    e = jnp.exp(sc - mx)
    es = jnp.exp(sink - mx)
    inv = 1.0 / (jnp.sum(e, axis=1, keepdims=True) + es)
    return e * inv, es * inv


def _swa_specs(L):
    prev = lambda w: pl.BlockSpec((BLOCK, w), lambda i: (jnp.maximum(i - 1, 0), 0))
    cur = lambda w: pl.BlockSpec((BLOCK, w), lambda i: (i, 0))
    sink = pl.BlockSpec(memory_space=pltpu.SMEM)
    return [cur(SW_W), prev(SB_W), cur(SB_W), prev(SB_W), cur(SB_W), sink]


def _swa_fwd(swq, swk4, swv4, sinks):
    L = swq.shape[0]

    def body(q_ref, kp_ref, kc_ref, vp_ref, vc_ref, sink_ref, o_ref):
        mask = _swa_mask(pl.program_id(0))
        k2 = jnp.concatenate([kp_ref[...], kc_ref[...]], axis=0)
        v2 = jnp.concatenate([vp_ref[...], vc_ref[...]], axis=0)
        for m in range(SW_W // BLOCK):
            g = m // 4
            sl = slice(m * BLOCK, (m + 1) * BLOCK)
            q = q_ref[:, sl]
            acc = jnp.zeros((BLOCK, BLOCK), F32)
            for par in range(2):
                kv = slice((2 * g + par) * BLOCK, (2 * g + par + 1) * BLOCK)
                p, _ = _swa_probs(q, k2[:, kv], mask, sink_ref[0, 2 * m + par])
                acc = acc + _dot(p.astype(BF16), v2[:, kv])
            o_ref[:, sl] = acc

    return pl.pallas_call(
        body, name="swa_fwd", grid=(L // BLOCK,), in_specs=_swa_specs(L),
        out_specs=pl.BlockSpec((BLOCK, SW_W), lambda i: (i, 0)),
        out_shape=jax.ShapeDtypeStruct((L, SW_W), F32),
        compiler_params=_cparams("parallel"),
    )(swq, swk4, swk4, swv4, swv4, sinks)


def _swa_bwd(swq, swk4, swv4, sinks, do_sw):
    L = swq.shape[0]

    def body(q_ref, kp_ref, kc_ref, vp_ref, vc_ref, sink_ref, do_ref, dq_ref, dkp_ref, dkc_ref, dvp_ref, dvc_ref, ds_ref):
        i = pl.program_id(0)

        @pl.when(i == 0)
        def _():
            ds_ref[...] = jnp.zeros_like(ds_ref)

        mask = _swa_mask(i)
        k2 = jnp.concatenate([kp_ref[...], kc_ref[...]], axis=0)
        v2 = jnp.concatenate([vp_ref[...], vc_ref[...]], axis=0)
        lane = lax.broadcasted_iota(jnp.int32, (8, BLOCK), 1)
        dsink = jnp.zeros((8, BLOCK), F32)
        for g in range(2):
            dkk = [jnp.zeros((2 * BLOCK, BLOCK), F32) for _ in range(2)]
            dvv = [jnp.zeros((2 * BLOCK, BLOCK), F32) for _ in range(2)]
            for m in range(4 * g, 4 * g + 4):
                sl = slice(m * BLOCK, (m + 1) * BLOCK)
                q = q_ref[:, sl]
                do_b = do_ref[:, sl].astype(BF16)
                dq = jnp.zeros((BLOCK, BLOCK), F32)
                for par in range(2):
                    kv = slice((2 * g + par) * BLOCK, (2 * g + par + 1) * BLOCK)
                    kk = k2[:, kv]
                    p, ps = _swa_probs(q, kk, mask, sink_ref[0, 2 * m + par])
                    dp = _dot_nt(do_b, v2[:, kv])
                    dd = jnp.sum(p * dp, axis=1, keepdims=True)
                    ds = (p * (dp - dd)).astype(BF16)
                    dsink = dsink - jnp.where(lane == 2 * m + par, jnp.sum(ps * dd), 0.0)
                    dq = dq + _dot(ds, kk)
                    dkk[par] = dkk[par] + _dot_tn(ds, q)
                    dvv[par] = dvv[par] + _dot_tn(p.astype(BF16), do_b)
                dq_ref[:, sl] = dq
            for par in range(2):
                kv = slice((2 * g + par) * BLOCK, (2 * g + par + 1) * BLOCK)
                dkp_ref[:, kv] = dkk[par][0:BLOCK]
                dkc_ref[:, kv] = dkk[par][BLOCK:2 * BLOCK]
                dvp_ref[:, kv] = dvv[par][0:BLOCK]
                dvc_ref[:, kv] = dvv[par][BLOCK:2 * BLOCK]
        ds_ref[...] += dsink

    part = pl.BlockSpec((BLOCK, SB_W), lambda i: (i, 0))
    row = pl.BlockSpec((BLOCK, SW_W), lambda i: (i, 0))
    pshape = jax.ShapeDtypeStruct((L, SB_W), F32)
    return pl.pallas_call(
        body, name="swa_bwd", grid=(L // BLOCK,), in_specs=_swa_specs(L) + [row],
        out_specs=[row, part, part, part, part, pl.BlockSpec((8, BLOCK), lambda i: (0, 0))],
        out_shape=[jax.ShapeDtypeStruct((L, SW_W), F32), pshape, pshape, pshape, pshape,
                   jax.ShapeDtypeStruct((8, BLOCK), F32)],
        compiler_params=_cparams("arbitrary"),
    )(swq, swk4, swk4, swv4, swv4, sinks, do_sw)


def _gate_specs(tm):
    return [pl.BlockSpec((tm, SW_W), lambda i: (i, 0)), pl.BlockSpec((tm, D_MODEL), lambda i: (i, 1)),
            pl.BlockSpec((tm, D_MODEL), lambda i: (i, 2)), pl.BlockSpec((tm, SB_W), lambda i: (i, 6))]


def _post_fwd(o_sb, o_sw, gate, h0, w_bsb, w_bsw, w_out):
    L = h0.shape[0]
    tm = _row_tile(L, 320)

    def body(osb, osw, swz, gsb, gsw, sbz, h0_ref, wb1, wb2, wo, h1, usb, usw, mrg, ysb, ysw):
        z1 = sbz[...]
        z2 = swz[...]
        u1 = (osb[...] * (z1 * jax.nn.sigmoid(z1))).astype(BF16)
        u2 = (osw[...] * (z2 * jax.nn.sigmoid(z2))).astype(BF16)
        y1 = _dot(u1, wb1[...])
        y2 = _dot(u2, wb2[...])
        merged = (jax.nn.sigmoid(gsb[...]) * y1 + jax.nn.sigmoid(gsw[...]) * y2).astype(BF16)
        h1[...] = h0_ref[...] + _dot(merged, wo[...])
        usb[...] = u1
        usw[...] = u2
        mrg[...] = merged
        ysb[...] = y1
        ysw[...] = y2

    row = lambda w: pl.BlockSpec((tm, w), lambda i: (i, 0))
    whole = lambda a: pl.BlockSpec(a.shape, lambda i: (0, 0))
    shp = lambda w, dt: jax.ShapeDtypeStruct((L, w), dt)
    return pl.pallas_call(
        body, name="post_fwd", grid=(L // tm,),
        in_specs=[row(SB_W), row(SW_W)] + _gate_specs(tm) + [row(D_MODEL), whole(w_bsb), whole(w_bsw), whole(w_out)],
        out_specs=[row(D_MODEL), row(SB_W), row(SW_W), row(D_MODEL), row(D_MODEL), row(D_MODEL)],
        out_shape=[shp(D_MODEL, F32), shp(SB_W, BF16), shp(SW_W, BF16), shp(D_MODEL, BF16),
                   shp(D_MODEL, F32), shp(D_MODEL, F32)],
        compiler_params=_cparams("parallel"),
    )(o_sb, o_sw, gate, gate, gate, gate, h0, w_bsb, w_bsw, w_out)


def _loss_bwd(h1, target, gain2):
    L = h1.shape[0]

    def body(h_ref, t_ref, g_ref, dh_ref, loss_ref, dg_ref):
        i = pl.program_id(0)

        @pl.when(i == 0)
        def _():
            dh_ref[...] = jnp.zeros_like(dh_ref)
            loss_ref[...] = jnp.zeros_like(loss_ref)
            dg_ref[...] = jnp.zeros_like(dg_ref)

        @pl.when(i > 0)
        def _():
            h = h_ref[...]
            g = g_ref[...]
            r = lax.rsqrt(jnp.mean(h * h, axis=-1, keepdims=True) + RMS_EPS)
            n = h * r
            err = n * g - t_ref[...]
            loss_ref[...] += 0.5 * jnp.sum(jnp.mean(err * err, axis=-1, keepdims=True))
            dy = err * (1.0 / D_MODEL)
            dg_ref[...] += jnp.sum(dy * n, axis=0, keepdims=True)
            dn = dy * g
            dh_ref[...] = r * (dn - n * jnp.mean(dn * n, axis=-1, keepdims=True))

    return pl.pallas_call(
        body, name="loss_bwd", grid=(L // BLOCK,),
        in_specs=[pl.BlockSpec((BLOCK, D_MODEL), lambda i: (i, 0)),
                  pl.BlockSpec((BLOCK, D_MODEL), lambda i: (jnp.maximum(i - 1, 0), 0)),
                  pl.BlockSpec((1, D_MODEL), lambda i: (0, 0))],
        out_specs=[pl.BlockSpec((BLOCK, D_MODEL), lambda i: (i, 0)), pl.BlockSpec((1, BLOCK), lambda i: (0, 0)),
                   pl.BlockSpec((1, D_MODEL), lambda i: (0, 0))],
        out_shape=[jax.ShapeDtypeStruct((L, D_MODEL), F32), jax.ShapeDtypeStruct((1, BLOCK), F32),
                   jax.ShapeDtypeStruct((1, D_MODEL), F32)],
        compiler_params=_cparams("arbitrary"),
    )(h1, target, gain2)


def _post_bwd(dh1, gate, y_sb, y_sw, o_sb, o_sw, wt_out, wt_bsb, wt_bsw):
    L = dh1.shape[0]
    tm = _row_tile(L, 320)

    def body(dh, swz, gsb, gsw, sbz, ysb, ysw, osb, osw, wo, wb1, wb2, dy1_ref, dy2_ref, do1_ref, do2_ref, dg_ref):
        dm = _dot(dh[...].astype(BF16), wo[...])
        s1 = jax.nn.sigmoid(gsb[...])
        s2 = jax.nn.sigmoid(gsw[...])
        dy1 = (dm * s1).astype(BF16)
        dy2 = (dm * s2).astype(BF16)
        dy1_ref[...] = dy1
        dy2_ref[...] = dy2
        du1 = _dot(dy1, wb1[...])
        du2 = _dot(dy2, wb2[...])
        z1 = sbz[...]
        z2 = swz[...]
        sz1 = jax.nn.sigmoid(z1)
        sz2 = jax.nn.sigmoid(z2)
        do1_ref[...] = du1 * (z1 * sz1)
        do2_ref[...] = du2 * (z2 * sz2)
        dg_ref[:, 0:SW_W] = (du2 * osw[...] * (sz2 * (1.0 + z2 * (1.0 - sz2)))).astype(BF16)
        dg_ref[:, SW_W:SW_W + D_MODEL] = (dm * ysb[...] * (s1 * (1.0 - s1))).astype(BF16)
        dg_ref[:, SW_W + D_MODEL:SW_W + 2 * D_MODEL] = (dm * ysw[...] * (s2 * (1.0 - s2))).astype(BF16)
        dg_ref[:, SW_W + 2 * D_MODEL:GATE_W] = (du1 * osb[...] * (sz1 * (1.0 + z1 * (1.0 - sz1)))).astype(BF16)

    row = lambda w: pl.BlockSpec((tm, w), lambda i: (i, 0))
    whole = lambda a: pl.BlockSpec(a.shape, lambda i: (0, 0))
    shp = lambda w, dt: jax.ShapeDtypeStruct((L, w), dt)
    return pl.pallas_call(
        body, name="post_bwd", grid=(L // tm,),
        in_specs=[row(D_MODEL)] + _gate_specs(tm) + [row(D_MODEL), row(D_MODEL), row(SB_W), row(SW_W),
                                                     whole(wt_out), whole(wt_bsb), whole(wt_bsw)],
        out_specs=[row(D_MODEL), row(D_MODEL), row(SB_W), row(SW_W), row(GATE_W)],
        out_shape=[shp(D_MODEL, BF16), shp(D_MODEL, BF16), shp(SB_W, F32), shp(SW_W, F32), shp(GATE_W, BF16)],
        compiler_params=_cparams("parallel"),
    )(dh1, gate, gate, gate, gate, y_sb, y_sw, o_sb, o_sw, wt_out, wt_bsb, wt_bsw)


def _assemble_dqkv(dswq, dsbq, dsbk, dsbv, dkp, dkc, dvp, dvc, cos_t, sin_t):
    L = dswq.shape[0]
    tm = BLOCK
    nb = L // BLOCK

    def body(dq_ref, dsq_ref, dsk_ref, dsv_ref, dkp_ref, dkc_ref, dvp_ref, dvc_ref, c_ref, s_ref, o_ref):
        C = c_ref[...]
        S = s_ref[...]
        lane = lax.broadcasted_iota(jnp.int32, (tm, BLOCK), 1)
        lo32 = lane % HEAD_DIM < HEAD_DIM // 2
        lo = lane < HEAD_DIM
        has_next = (pl.program_id(0) + 1 < nb).astype(F32)

        def unrope(dy):
            return dy * C + _rope_partner(dy * S, lo32)

        def fold(cur_ref, next_ref):
            b00, b01, b10, b11 = (cur_ref[:, k * BLOCK:(k + 1) * BLOCK] + has_next * next_ref[:, k * BLOCK:(k + 1) * BLOCK]
                                  for k in range(4))
            return jnp.where(lo, b00 + pltpu.roll(b01, HEAD_DIM, 1), pltpu.roll(b10, HEAD_DIM, 1) + b11)

        for m in range(SW_W // BLOCK):
            sl = slice(m * BLOCK, (m + 1) * BLOCK)
            o_ref[:, sl] = unrope(dq_ref[:, sl] * Q_SCALE).astype(BF16)
        o_ref[:, SW_W:SW_W + SB_W] = (dsq_ref[...] * Q_SCALE).astype(BF16)
        o_ref[:, SW_W + SB_W:SW_W + 2 * SB_W] = dsk_ref[...].astype(BF16)
        o_ref[:, SW_W + 2 * SB_W:SW_W + 3 * SB_W] = dsv_ref[...].astype(BF16)
        base = SW_W + 3 * SB_W
        o_ref[:, base:base + KV_W] = unrope(fold(dkc_ref, dkp_ref)).astype(BF16)
        o_ref[:, base + KV_W:base + 2 * KV_W] = fold(dvc_ref, dvp_ref).astype(BF16)

    row = lambda w: pl.BlockSpec((tm, w), lambda i: (i, 0))
    nxt = pl.BlockSpec((tm, SB_W), lambda i: (jnp.minimum(i + 1, nb - 1), 0))
    return pl.pallas_call(
        body, name="assemble_dqkv", grid=(nb,),
        in_specs=[row(SW_W), row(SB_W), row(SB_W), row(SB_W), nxt, row(SB_W), nxt, row(SB_W), row(BLOCK), row(BLOCK)],
        out_specs=row(QKV_W), out_shape=jax.ShapeDtypeStruct((L, QKV_W), BF16),
        compiler_params=_cparams("parallel"),
    )(dswq, dsbq, dsbk, dsbv, dkp, dkc, dvp, dvc, cos_t, sin_t)


def _norm_bwd(dxn, h0, dh1, gain):
    L = h0.shape[0]

    def body(dx_ref, h_ref, dh_ref, g_ref, gx_ref, dm_ref, dg_ref):
        i = pl.program_id(0)

        @pl.when(i == 0)
        def _():
            dg_ref[...] = jnp.zeros_like(dg_ref)

        h = h_ref[...]
        dxn_t = dx_ref[...]
        r = lax.rsqrt(jnp.mean(h * h, axis=-1, keepdims=True) + RMS_EPS)
        n = h * r
        dg_ref[...] += jnp.sum(dxn_t * n, axis=0, keepdims=True)
        dn = dxn_t * g_ref[...]
        dh0 = dh_ref[...] + r * (dn - n * jnp.mean(dn * n, axis=-1, keepdims=True))
        gx_ref[...] = dh0

        @pl.when(i == 0)
        def _():
            dm_ref[...] = dh0[PAD:BLOCK]

    row = pl.BlockSpec((BLOCK, D_MODEL), lambda i: (i, 0))
    return pl.pallas_call(
        body, name="norm_bwd", grid=(L // BLOCK,),
        in_specs=[row, row, row, pl.BlockSpec((1, D_MODEL), lambda i: (0, 0))],
        out_specs=[pl.BlockSpec((BLOCK, D_MODEL), lambda i: (jnp.maximum(i - 1, 0), 0)),
                   pl.BlockSpec((N_META, D_MODEL), lambda i: (0, 0)), pl.BlockSpec((1, D_MODEL), lambda i: (0, 0))],
        out_shape=[jax.ShapeDtypeStruct((L - BLOCK, D_MODEL), F32), jax.ShapeDtypeStruct((N_META, D_MODEL), F32),
                   jax.ShapeDtypeStruct((1, D_MODEL), F32)],
        compiler_params=_cparams("arbitrary"),
    )(dxn, h0, dh1, gain)


def _adamw(w, g, m, v, name):
    R, C = w.shape
    tr = _row_tile(R, 256) if R % 16 == 0 else R

    def body(w_ref, g_ref, m_ref, v_ref, d_ref, nm_ref, nv_ref):
        g = g_ref[...]
        m_new = ADAM_B1 * m_ref[...] + (1.0 - ADAM_B1) * g
        v_new = ADAM_B2 * v_ref[...] + (1.0 - ADAM_B2) * (g * g)
        m_hat = m_new / (1.0 - ADAM_B1 ** ADAM_STEP)
        v_hat = v_new / (1.0 - ADAM_B2 ** ADAM_STEP)
        d_ref[...] = -ADAM_LR * (m_hat / (jnp.sqrt(v_hat) + ADAM_EPS) + ADAM_WD * w_ref[...])
        nm_ref[...] = m_new
        nv_ref[...] = v_new

    spec = pl.BlockSpec((tr, C), lambda i: (i, 0))
    shp = jax.ShapeDtypeStruct((R, C), F32)
    return pl.pallas_call(
        body, name=name, grid=(R // tr,), in_specs=[spec] * 4, out_specs=[spec] * 3, out_shape=[shp] * 3,
        compiler_params=_cparams("parallel"),
    )(w, g, m, v)


def _sum_leading(a, name):
    n, R, C = a.shape
    tr = _row_tile(R, 160) if R % 16 == 0 else R

    def body(a_ref, o_ref):
        acc = a_ref[0]
        for k in range(1, n):
            acc = acc + a_ref[k]
        o_ref[...] = acc

    return pl.pallas_call(
        body, name=name, grid=(R // tr,), in_specs=[pl.BlockSpec((n, tr, C), lambda i: (0, i, 0))],
        out_specs=pl.BlockSpec((tr, C), lambda i: (i, 0)), out_shape=jax.ShapeDtypeStruct((R, C), a.dtype),
        compiler_params=_cparams("parallel"),
    )(a)


def _place():
    x, y, c = lax.axis_index("x"), lax.axis_index("y"), lax.axis_index("c")
    return x, y, c


def _all_gather(block, name):
    R, C = block.shape

    def body(x_ref, out_ref, send_sems, recv_sems, local_sem):
        x, y, c = _place()
        me, sibling = (x, y, c), (x, y, 1 - c)
        chips = [(1 - x, y), (x, 1 - y), (1 - x, 1 - y)]

        def slot(px, py, pc):
            return out_ref.at[4 * px + 2 * py + pc]

        def copy(k, blk, to, src=None):
            return pltpu.make_async_remote_copy(
                src_ref=slot(*blk) if src is None else src, dst_ref=slot(*blk),
                send_sem=send_sems.at[k], recv_sem=recv_sems.at[k], device_id=to, device_id_type=MESH)

        mine = pltpu.make_async_copy(x_ref, slot(*me), local_sem)
        mine.start()
        first = [copy(0, me, sibling, src=x_ref)]
        first += [copy(1 + j, me, (*chip, c), src=x_ref) for j, chip in enumerate(chips)]
        for cp in first:
            cp.start()
        passed = [copy(4 + j, (*chip, c), sibling) for j, chip in enumerate(chips)]
        for j, chip in enumerate(chips):
            copy(1 + j, (*chip, c), me).wait_recv()
            passed[j].start()
        copy(0, sibling, me).wait_recv()
        for j, chip in enumerate(chips):
            copy(4 + j, (*chip, 1 - c), me).wait_recv()
        for cp in first + passed:
            cp.wait_send()
        mine.wait()

    return pl.pallas_call(
        body, name=name, in_specs=[ANY], out_specs=ANY,
        out_shape=jax.ShapeDtypeStruct((N_DEV, R, C), block.dtype),
        scratch_shapes=[pltpu.SemaphoreType.DMA((7,)), pltpu.SemaphoreType.DMA((7,)), pltpu.SemaphoreType.DMA],
    )(block)


def _pair_exchange(g8):
    _, R, C = g8.shape

    def body(g_ref, out_ref, send_sems, recv_sems, local_sems):
        x, y, c = _place()
        sibling = (x, y, 1 - c)
        local = [pltpu.make_async_copy(g_ref.at[2 * s + c], out_ref.at[c, s], local_sems.at[s]) for s in range(N_CHIP)]
        remote = [pltpu.make_async_remote_copy(
            src_ref=g_ref.at[2 * s + (1 - c)], dst_ref=out_ref.at[c, s], send_sem=send_sems.at[s],
            recv_sem=recv_sems.at[s], device_id=sibling, device_id_type=MESH) for s in range(N_CHIP)]
        for cp in remote + local:
            cp.start()
        for s in range(N_CHIP):
            pltpu.make_async_remote_copy(
                src_ref=g_ref.at[s], dst_ref=out_ref.at[1 - c, s], send_sem=send_sems.at[s],
                recv_sem=recv_sems.at[s], device_id=sibling, device_id_type=MESH).wait_recv()
        for cp in remote:
            cp.wait_send()
        for cp in local:
            cp.wait()

    return pl.pallas_call(
        body, name="grad_pair_exchange", in_specs=[ANY], out_specs=ANY,
        out_shape=jax.ShapeDtypeStruct((2, N_CHIP, R, C), g8.dtype),
        scratch_shapes=[pltpu.SemaphoreType.DMA((N_CHIP,)), pltpu.SemaphoreType.DMA((N_CHIP,)),
                        pltpu.SemaphoreType.DMA((N_CHIP,))],
    )(g8)


def _chip_scatter(p4):
    _, R, C = p4.shape

    def body(p_ref, out_ref, send_sems, recv_sems, local_sem):
        x, y, c = _place()
        my_chip = 2 * x + y
        chips = [(1 - x, y), (x, 1 - y), (1 - x, 1 - y)]
        local = pltpu.make_async_copy(p_ref.at[my_chip], out_ref.at[my_chip], local_sem)
        local.start()
        sends = [pltpu.make_async_remote_copy(
            src_ref=p_ref.at[2 * cx + cy], dst_ref=out_ref.at[my_chip], send_sem=send_sems.at[j],
            recv_sem=recv_sems.at[j], device_id=(cx, cy, c), device_id_type=MESH) for j, (cx, cy) in enumerate(chips)]
        for cp in sends:
            cp.start()
        for j, (cx, cy) in enumerate(chips):
            pltpu.make_async_remote_copy(
                src_ref=p_ref.at[my_chip], dst_ref=out_ref.at[2 * cx + cy], send_sem=send_sems.at[j],
                recv_sem=recv_sems.at[j], device_id=(cx, cy, c), device_id_type=MESH).wait_recv()
        for cp in sends:
            cp.wait_send()
        local.wait()

    return pl.pallas_call(
        body, name="grad_chip_scatter", in_specs=[ANY], out_specs=ANY,
        out_shape=jax.ShapeDtypeStruct((N_CHIP, R, C), p4.dtype),
        scratch_shapes=[pltpu.SemaphoreType.DMA((3,)), pltpu.SemaphoreType.DMA((3,)), pltpu.SemaphoreType.DMA],
    )(p4)


def _pair_share(r):
    R, C = r.shape

    def body(r_ref, out_ref, send_sem, recv_sem, local_sem):
        x, y, c = _place()
        sibling = (x, y, 1 - c)
        local = pltpu.make_async_copy(r_ref, out_ref.at[c], local_sem)
        local.start()
        send = pltpu.make_async_remote_copy(src_ref=r_ref, dst_ref=out_ref.at[c], send_sem=send_sem,
                                            recv_sem=recv_sem, device_id=sibling, device_id_type=MESH)
        send.start()
        pltpu.make_async_remote_copy(src_ref=r_ref, dst_ref=out_ref.at[1 - c], send_sem=send_sem,
                                     recv_sem=recv_sem, device_id=sibling, device_id_type=MESH).wait_recv()
        send.wait_send()
        local.wait()

    return pl.pallas_call(
        body, name="grad_pair_share", in_specs=[ANY], out_specs=ANY,
        out_shape=jax.ShapeDtypeStruct((2, R, C), r.dtype),
        scratch_shapes=[pltpu.SemaphoreType.DMA, pltpu.SemaphoreType.DMA, pltpu.SemaphoreType.DMA],
    )(r)


def _gather_direct(block, name):
    R, C = block.shape

    def body(x_ref, out_ref, send_sems, recv_sems, local_sem):
        x, y, c = _place()
        me = 4 * x + 2 * y + c
        local = pltpu.make_async_copy(x_ref, out_ref.at[me], local_sem)
        local.start()
        peers = []
        for k in range(1, N_DEV):
            px, py, pc = x ^ (k >> 2), y ^ ((k >> 1) & 1), c ^ (k & 1)
            peers.append((k, (px, py, pc), 4 * px + 2 * py + pc))
        sends = [pltpu.make_async_remote_copy(
            src_ref=x_ref, dst_ref=out_ref.at[me], send_sem=send_sems.at[k - 1], recv_sem=recv_sems.at[k - 1],
            device_id=dev, device_id_type=MESH) for k, dev, _ in peers]
        for cp in sends:
            cp.start()
        for k, dev, idx in peers:
            pltpu.make_async_remote_copy(
                src_ref=x_ref, dst_ref=out_ref.at[idx], send_sem=send_sems.at[k - 1], recv_sem=recv_sems.at[k - 1],
                device_id=dev, device_id_type=MESH).wait_recv()
        for cp in sends:
            cp.wait_send()
        local.wait()

    return pl.pallas_call(
        body, name=name, in_specs=[ANY], out_specs=ANY,
        out_shape=jax.ShapeDtypeStruct((N_DEV, R, C), block.dtype),
        scratch_shapes=[pltpu.SemaphoreType.DMA((7,)), pltpu.SemaphoreType.DMA((7,)), pltpu.SemaphoreType.DMA],
    )(block)


W_IN_SHARD = IN_COLS // N_CHIP
ROWS_W_IN = (D_MODEL // 2) * W_IN_SHARD // D_MODEL
ROWS_BSB = (SB_W // 2) * (D_MODEL // N_CHIP) // D_MODEL
ROWS_SQ = D_MODEL // N_CHIP // 2
ROWS_BIG = ROWS_W_IN + ROWS_BSB + 2 * ROWS_SQ


def _half(a, c, axis):
    n = a.shape[axis] // 2
    return lax.dynamic_slice_in_dim(a, c * n, n, axis)


def _pack_shard_half(w_in, w_bsb, w_bsw, w_out, c):
    parts = [_half(w_in, c, 0).reshape(ROWS_W_IN, D_MODEL), _half(w_bsb, c, 0).reshape(ROWS_BSB, D_MODEL),
             _half(w_bsw, c, 0), _half(w_out, c, 0)]
    return jnp.concatenate(parts, axis=0)


def _unpack_full(blocks):
    b = blocks.reshape(N_CHIP, 2, ROWS_BIG, D_MODEL)
    o = 0
    w_in = b[:, :, o:o + ROWS_W_IN].reshape(N_CHIP, D_MODEL, W_IN_SHARD)
    o += ROWS_W_IN
    w_bsb = b[:, :, o:o + ROWS_BSB].reshape(N_CHIP, SB_W, D_MODEL // N_CHIP)
    o += ROWS_BSB
    w_bsw = b[:, :, o:o + ROWS_SQ].reshape(D_MODEL, D_MODEL)
    o += ROWS_SQ
    w_out = b[:, :, o:o + ROWS_SQ].reshape(D_MODEL, D_MODEL)
    w_in = jnp.transpose(w_in, (1, 0, 2)).reshape(D_MODEL, IN_COLS)
    w_bsb = jnp.transpose(w_bsb, (1, 0, 2)).reshape(SB_W, D_MODEL)
    return w_in, w_bsb, w_bsw, w_out


def _pack_grads(dw_in, dw_bsb, dw_bsw, dw_out):
    a = jnp.transpose(dw_in.reshape(2, D_MODEL // 2, N_CHIP, W_IN_SHARD), (2, 0, 1, 3)).reshape(N_CHIP, 2, ROWS_W_IN, D_MODEL)
    b = jnp.transpose(dw_bsb.reshape(2, SB_W // 2, N_CHIP, D_MODEL // N_CHIP), (2, 0, 1, 3)).reshape(N_CHIP, 2, ROWS_BSB, D_MODEL)
    c = dw_bsw.reshape(N_CHIP, 2, ROWS_SQ, D_MODEL)
    d = dw_out.reshape(N_CHIP, 2, ROWS_SQ, D_MODEL)
    return jnp.concatenate([a, b, c, d], axis=2).reshape(N_DEV, ROWS_BIG, D_MODEL)


def _unpack_shard(full2):
    o = 0
    w_in = full2[:, o:o + ROWS_W_IN].reshape(D_MODEL, W_IN_SHARD)
    o += ROWS_W_IN
    w_bsb = full2[:, o:o + ROWS_BSB].reshape(SB_W, D_MODEL // N_CHIP)
    o += ROWS_BSB
    w_bsw = full2[:, o:o + ROWS_SQ].reshape(D_MODEL // N_CHIP, D_MODEL)
    o += ROWS_SQ
    w_out = full2[:, o:o + ROWS_SQ].reshape(D_MODEL // N_CHIP, D_MODEL)
    return w_in, w_bsb, w_bsw, w_out


_QKV_FROM_IN = ((1536, 2560), (0, 512), (512, 1024), (1024, 1536), (2560, 2688), (2688, 2816))
_GATE_FROM_IN = ((3328, 4352), (4352, 5376), (5376, 6400), (2816, 3328))


def _split_w_in(w_in):
    qkv = jnp.concatenate([w_in[:, a:b] for a, b in _QKV_FROM_IN], axis=1)
    gate = jnp.concatenate([w_in[:, a:b] for a, b in _GATE_FROM_IN], axis=1)
    return qkv, gate


def _join_dw_in(dqkv, dgate):
    q = lambda a, b: dqkv[:, a:b]
    g = lambda a, b: dgate[:, a:b]
    return jnp.concatenate([q(1024, 1536), q(1536, 2048), q(2048, 2560), q(0, 1024), q(2560, 2688), q(2688, 2816),
                            g(3072, 3584), g(0, 1024), g(1024, 2048), g(2048, 3072)], axis=1)


def _rope_tables(L):
    half = HEAD_DIM // 2
    inv = ROPE_THETA ** (-jnp.arange(half, dtype=F32) / half)
    pos = (jnp.arange(L) - PAD).astype(F32)
    ang = pos[:, None] * inv[None, :]
    cos, sin = jnp.cos(ang), jnp.sin(ang)
    return jnp.tile(jnp.concatenate([cos, cos], axis=1), (1, 2)), jnp.tile(jnp.concatenate([-sin, sin], axis=1), (1, 2))


def kernel(x, meta_tokens, norm_gain, w_in, w_branch_sb, w_branch_swa, w_out, attn_sinks, final_norm_gain, loss_target, m_meta_tokens, m_norm_gain, m_w_in, m_w_branch_sb, m_w_branch_swa, m_w_out, m_attn_sinks, m_final_norm_gain, v_meta_tokens, v_norm_gain, v_w_in, v_w_branch_sb, v_w_branch_swa, v_w_out, v_attn_sinks, v_final_norm_gain):
    xi, yi, ci = _place()
    chip = 2 * xi + yi
    seq = x.shape[1]
    L = seq + BLOCK

    wpack = _pack_shard_half(w_in[0], w_branch_sb[0], w_branch_swa[0], w_out[0], ci).astype(BF16)
    w_in_f, w_bsb_f, w_bsw_f, w_out_f = _unpack_full(_all_gather(wpack, "weight_all_gather"))
    mg = _all_gather(_half(meta_tokens, ci, 0), "meta_all_gather").reshape(N_CHIP, 2, N_META // 2, D_MODEL // N_CHIP)
    meta_full = jnp.transpose(mg, (1, 2, 0, 3)).reshape(N_META, D_MODEL)
    w_qkv, w_gate = _split_w_in(w_in_f)

    h0 = jnp.concatenate([jnp.zeros((PAD, D_MODEL), F32), meta_full, x[0]], axis=0)
    cos_t, sin_t = _rope_tables(L)
    xn = _norm_fwd(h0, norm_gain)
    qkv = _mm([(xn, w_qkv)], F32, "in_proj_qkv", tn_pref=1408)
    gate = _mm([(xn, w_gate)], F32, "in_proj_gate")
    swq, sbq, kcat, vcat, swk4, swv4 = _prep_qkv(qkv, cos_t, sin_t)
    o_sb = _sb_fwd(sbq, kcat, vcat)
    o_sw = _swa_fwd(swq, swk4, swv4, attn_sinks)
    h1, u_sb, u_sw, merged, y_sb, y_sw = _post_fwd(o_sb, o_sw, gate, h0, w_bsb_f, w_bsw_f, w_out_f)

    dh1, loss_part, dgain2 = _loss_bwd(h1, loss_target[0], final_norm_gain.reshape(1, D_MODEL))
    dy_sb, dy_sw, do_sb, do_sw, dgate = _post_bwd(dh1, gate, y_sb, y_sw, o_sb, o_sw, w_out_f.T, w_bsb_f.T, w_bsw_f.T)
    dsbq, dsbk, dsbv = _sb_bwd(sbq, kcat, vcat, do_sb)
    dswq, dkp, dkc, dvp, dvc, dsink = _swa_bwd(swq, swk4, swv4, attn_sinks, do_sw)
    dqkv = _assemble_dqkv(dswq, dsbq, dsbk, dsbv, dkp, dkc, dvp, dvc, cos_t, sin_t)
    dxn = _mm([(dqkv, w_qkv.T), (dgate, w_gate.T)], F32, "dxn", tm_pref=320, tn_pref=1024)
    grad_x, dmeta, dgain1 = _norm_bwd(dxn, h0, dh1, norm_gain)
    dw_qkv = _mm_tn(xn, dqkv, "dw_qkv", tn_pref=1408)
    dw_gate = _mm_tn(xn, dgate, "dw_gate")
    dw_bsb = _mm_tn(u_sb, dy_sb, "dw_branch_sb", tn_pref=1024)
    dw_bsw = _mm_tn(u_sw, dy_sw, "dw_branch_swa", tn_pref=1024)
    dw_out = _mm_tn(merged, dh1, "dw_out", tn_pref=1024)

    g8 = _pack_grads(_join_dw_in(dw_qkv, dw_gate), dw_bsb, dw_bsw, dw_out)
    pair = _pair_exchange(g8)
    p4 = _sum_leading(pair.reshape(2, N_CHIP * ROWS_BIG, D_MODEL), "grad_pair_sum").reshape(N_CHIP, ROWS_BIG, D_MODEL)
    r_half = _sum_leading(_chip_scatter(p4), "grad_chip_sum")
    g_in, g_bsb, g_bsw, g_out = _unpack_shard(_pair_share(r_half))

    small = jnp.concatenate([dmeta, dgain1, dgain2, jnp.pad(dsink[0:1], ((0, 0), (0, D_MODEL - BLOCK))),
                             jnp.pad(loss_part, ((0, 0), (0, D_MODEL - BLOCK))),
                             jnp.zeros((32 - N_META - 4, D_MODEL), F32)], axis=0)
    tot = _sum_leading(_gather_direct(small, "small_all_gather"), "small_sum")
    g_meta = lax.dynamic_slice_in_dim(tot[0:N_META], chip * (D_MODEL // N_CHIP), D_MODEL // N_CHIP, 1)
    g_gain1 = tot[N_META:N_META + 1]
    g_gain2 = tot[N_META + 1]
    g_sinks = tot[N_META + 2:N_META + 3, 0:16]
    loss = tot[N_META + 3, 0]

    def upd(w, g, m, v, name):
        shape = w.shape
        as2d = lambda a: a.reshape(-1, shape[-1])
        d, nm, nv = _adamw(as2d(w), as2d(g), as2d(m), as2d(v), name)
        return g.reshape(shape), d.reshape(shape), nm.reshape(shape), nv.reshape(shape)

    res = [
        upd(meta_tokens, g_meta, m_meta_tokens, v_meta_tokens, "adamw_meta"),
        upd(norm_gain, g_gain1, m_norm_gain, v_norm_gain, "adamw_norm_gain"),
        upd(w_in, g_in, m_w_in, v_w_in, "adamw_w_in"),
        upd(w_branch_sb, g_bsb, m_w_branch_sb, v_w_branch_sb, "adamw_w_branch_sb"),
        upd(w_branch_swa, g_bsw, m_w_branch_swa, v_w_branch_swa, "adamw_w_branch_swa"),
        upd(w_out, g_out, m_w_out, v_w_out, "adamw_w_out"),
        upd(attn_sinks, g_sinks, m_attn_sinks, v_attn_sinks, "adamw_attn_sinks"),
        upd(final_norm_gain, g_gain2, m_final_norm_gain, v_final_norm_gain, "adamw_final_norm_gain"),
    ]
    grads, deltas, new_m, new_v = zip(*res)
    return (loss, grad_x.reshape(1, seq, D_MODEL), *grads, *deltas, *new_m, *new_v)
```

```python
import functools

import jax
import jax.numpy as jnp
from jax import lax
from jax.experimental import pallas as pl
from jax.experimental.pallas import tpu as pltpu

F32 = jnp.float32
BF16 = jnp.bfloat16

D_MODEL = 1024
BLOCK = 128
N_META = 16
PAD = BLOCK - N_META
HEAD_DIM = 64
SB_W = 512
SW_W = 1024
KV_W = 128
QKV_W = SW_W + 3 * SB_W + 2 * KV_W
GATE_W = 2 * D_MODEL + SW_W + SB_W
IN_COLS = QKV_W + GATE_W
ROPE_THETA = 10000.0
RMS_EPS = 1e-6
Q_SCALE = HEAD_DIM ** -0.5

ADAM_LR = 0.001
ADAM_B1 = 0.9
ADAM_B2 = 0.999
ADAM_EPS = 1e-08
ADAM_WD = 0.01
ADAM_STEP = 10

N_DEV = 8
N_CHIP = 4
MESH = pl.DeviceIdType.MESH
VMEM_LIMIT = 56 * 1024 * 1024
ANY = pl.BlockSpec(memory_space=pl.ANY)


def _cparams(*sem):
    return pltpu.CompilerParams(dimension_semantics=sem, vmem_limit_bytes=VMEM_LIMIT)


def _row_tile(n, pref):
    best = None
    for t in range(16, pref + 1, 16):
        if n % t == 0:
            best = t
    assert best is not None, (n, pref)
    return best


def _col_tile(n, pref):
    best = None
    for t in range(128, pref + 1, 128):
        if n % t == 0:
            best = t
    assert best is not None, (n, pref)
    return best


def _dot(a, b):
    return jnp.dot(a, b, preferred_element_type=F32)


def _dot_nt(a, b):
    return lax.dot_general(a, b, (((1,), (1,)), ((), ())), preferred_element_type=F32)


def _dot_tn(a, b):
    return lax.dot_general(a, b, (((0,), (0,)), ((), ())), preferred_element_type=F32)


def _lane_lo(shape):
    return lax.broadcasted_iota(jnp.int32, shape, len(shape) - 1) % BLOCK < HEAD_DIM


def _mm(pairs, out_dtype, name, tm_pref=640, tn_pref=1792):
    M = pairs[0][0].shape[0]
    N = pairs[0][1].shape[1]
    tm = _row_tile(M, tm_pref)
    tn = _col_tile(N, tn_pref)
    n_pairs = len(pairs)

    def body(*refs):
        o_ref = refs[-1]
        acc = None
        for p in range(n_pairs):
            d = _dot(refs[2 * p][...].astype(BF16), refs[2 * p + 1][...])
            acc = d if acc is None else acc + d
        o_ref[...] = acc.astype(out_dtype)

    in_specs, args = [], []
    for a, b in pairs:
        k = a.shape[1]
        in_specs += [pl.BlockSpec((tm, k), lambda n, m: (m, 0)), pl.BlockSpec((k, tn), lambda n, m: (0, n))]
        args += [a, b]
    return pl.pallas_call(
        body, name=name, grid=(N // tn, M // tm), in_specs=in_specs,
        out_specs=pl.BlockSpec((tm, tn), lambda n, m: (m, n)),
        out_shape=jax.ShapeDtypeStruct((M, N), out_dtype),
        compiler_params=_cparams("parallel", "arbitrary"),
    )(*args)


def _mm_tn(a, b, name, tn_pref=1792, tl_pref=640):
    L, M = a.shape
    N = b.shape[1]
    tn = _col_tile(N, tn_pref)
    tl = _row_tile(L, tl_pref)

    def body(a_ref, b_ref, o_ref):
        @pl.when(pl.program_id(1) == 0)
        def _():
            o_ref[...] = jnp.zeros_like(o_ref)
        o_ref[...] += _dot_tn(a_ref[...].astype(BF16), b_ref[...].astype(BF16))

    return pl.pallas_call(
        body, name=name, grid=(N // tn, L // tl),
        in_specs=[pl.BlockSpec((tl, M), lambda n, l: (l, 0)), pl.BlockSpec((tl, tn), lambda n, l: (l, n))],
        out_specs=pl.BlockSpec((M, tn), lambda n, l: (0, n)),
        out_shape=jax.ShapeDtypeStruct((M, N), F32),
        compiler_params=_cparams("parallel", "arbitrary"),
    )(a, b)


def _norm_fwd(h0, gain):
    L = h0.shape[0]
    tm = _row_tile(L, 640)

    def body(h_ref, g_ref, o_ref):
        h = h_ref[...]
        r = lax.rsqrt(jnp.mean(h * h, axis=-1, keepdims=True) + RMS_EPS)
        o_ref[...] = ((h * r) * g_ref[...]).astype(BF16)

    return pl.pallas_call(
        body, name="norm_fwd", grid=(L // tm,),
        in_specs=[pl.BlockSpec((tm, D_MODEL), lambda i: (i, 0)), pl.BlockSpec((1, D_MODEL), lambda i: (0, 0))],
        out_specs=pl.BlockSpec((tm, D_MODEL), lambda i: (i, 0)),
        out_shape=jax.ShapeDtypeStruct((L, D_MODEL), BF16),
        compiler_params=_cparams("parallel"),
    )(h0, gain)


def _rope_partner(x, lo32):
    return jnp.where(lo32, pltpu.roll(x, 96, 1), pltpu.roll(x, 32, 1))


def _prep_qkv(qkv, cos_t, sin_t):
    L = qkv.shape[0]
    tm = BLOCK * max(d for d in range(1, 6) if (L // BLOCK) % d == 0)
    n_steps = L // tm
    assert L // BLOCK + tm // BLOCK >= _padded_blocks(L)

    def body(x_ref, c_ref, s_ref, swq, sbq, kcat, vcat, swk4, swv4):
        C = c_ref[...]
        S = s_ref[...]
        lane = lax.broadcasted_iota(jnp.int32, (tm, BLOCK), 1)
        lo32 = lane % HEAD_DIM < HEAD_DIM // 2
        lo = lane < HEAD_DIM

        def rope(x):
            return x * C + _rope_partner(x, lo32) * S

        for m in range(SW_W // BLOCK):
            sl = slice(m * BLOCK, (m + 1) * BLOCK)
            swq[:, sl] = (rope(x_ref[:, sl]) * Q_SCALE).astype(BF16)
        for m in range(SB_W // BLOCK):
            sl = slice(m * BLOCK, (m + 1) * BLOCK)
            sbq[:, sl] = (x_ref[:, SW_W + m * BLOCK:SW_W + (m + 1) * BLOCK] * Q_SCALE).astype(BF16)
            k = x_ref[:, SW_W + SB_W + m * BLOCK:SW_W + SB_W + (m + 1) * BLOCK]
            v = x_ref[:, SW_W + 2 * SB_W + m * BLOCK:SW_W + 2 * SB_W + (m + 1) * BLOCK]
            for src, dst in ((k, kcat), (v, vcat)):
                even = jnp.where(lo, src, 0.0).astype(BF16)
                odd = jnp.where(lo, 0.0, src).astype(BF16)
                for b in range(tm // BLOCK):
                    dst[2 * b * BLOCK:(2 * b + 1) * BLOCK, sl] = even[b * BLOCK:(b + 1) * BLOCK]
                    dst[(2 * b + 1) * BLOCK:(2 * b + 2) * BLOCK, sl] = odd[b * BLOCK:(b + 1) * BLOCK]
        base = SW_W + 3 * SB_W
        kx = rope(x_ref[:, base:base + KV_W])
        vx = x_ref[:, base + KV_W:base + 2 * KV_W]
        for src, dst in ((kx, swk4), (vx, swv4)):
            sw = pltpu.roll(src, HEAD_DIM, 1)
            dst[:, 0:128] = jnp.where(lo, src, 0.0).astype(BF16)
            dst[:, 128:256] = jnp.where(lo, 0.0, sw).astype(BF16)
            dst[:, 256:384] = jnp.where(lo, sw, 0.0).astype(BF16)
            dst[:, 384:512] = jnp.where(lo, 0.0, src).astype(BF16)

        @pl.when(pl.program_id(0) == n_steps)
        def _():
            kcat[...] = jnp.zeros_like(kcat)
            vcat[...] = jnp.zeros_like(vcat)

    row = lambda w: pl.BlockSpec((tm, w), lambda i: (jnp.minimum(i, n_steps - 1), 0))
    row2 = pl.BlockSpec((2 * tm, SB_W), lambda i: (i, 0))
    shp = lambda r, w: jax.ShapeDtypeStruct((r, w), BF16)
    return pl.pallas_call(
        body, name="prep_qkv", grid=(n_steps + 1,),
        in_specs=[row(QKV_W), row(BLOCK), row(BLOCK)],
        out_specs=[row(SW_W), row(SB_W), row2, row2, row(SB_W), row(SB_W)],
        out_shape=[shp(L, SW_W), shp(L, SB_W), shp(2 * (L + tm), SB_W), shp(2 * (L + tm), SB_W), shp(L, SB_W),
                   shp(L, SB_W)],
        compiler_params=_cparams("arbitrary"),
    )(qkv, cos_t, sin_t)


WIDE = 2 * BLOCK
GROUP = 4
SUPER = GROUP * WIDE


def _padded_blocks(L):
    return -(-(L // BLOCK) // GROUP) * GROUP


def _cumsum_matrices():
    r = lax.broadcasted_iota(jnp.int32, (WIDE, WIDE), 0)
    c = lax.broadcasted_iota(jnp.int32, (WIDE, WIDE), 1)
    same = (r < BLOCK) == (c < BLOCK)
    rev = jnp.where(same & (r >= c), 1.0, 0.0).astype(BF16)
    fwd = jnp.where(same & (r <= c), 1.0, 0.0).astype(BF16)
    return rev, fwd


def _head_totals(c, col):
    half = lax.broadcasted_iota(jnp.int32, c.shape, 1) < BLOCK
    return jnp.where(half, c[:, col:col + 1], c[:, BLOCK + col:BLOCK + col + 1])


def _chunks(x):
    return [x[:, t * WIDE:(t + 1) * WIDE] for t in range(GROUP)]


def _sb_diag_mask(i):
    t = i * BLOCK + lax.broadcasted_iota(jnp.int32, (BLOCK, SUPER), 0)
    lane = lax.broadcasted_iota(jnp.int32, (BLOCK, SUPER), 1)
    s = ((i // GROUP) * GROUP + lane // WIDE) * BLOCK + lane % BLOCK
    return (s < t) & (s >= PAD)


def _sb_logits(q, kc, mask):
    z = _dot_nt(q, kc)
    sp = jnp.log(1.0 + jnp.exp(-jnp.abs(z)))
    lb = jnp.minimum(z, 0.0) - sp
    l1m = lb - z
    if mask is not None:
        l1m = jnp.where(mask, l1m, 0.0)
    hi = l1m.astype(BF16)
    lo = (l1m - hi.astype(F32)).astype(BF16)
    return z, hi, lo, lb


def _sb_suffix(z, hi, lo, U_rev):
    cs = [_dot(h, U_rev) + _dot(l, U_rev) for h, l in zip(_chunks(hi), _chunks(lo))]
    ys, run = [None] * GROUP, None
    for t in reversed(range(GROUP)):
        y = z[:, t * WIDE:(t + 1) * WIDE] + cs[t]
        ys[t] = y if run is None else y + run
        tot = _head_totals(cs[t], 0)
        run = tot if run is None else run + tot
    return jnp.concatenate(ys, axis=1), run


def _super_rows(J):
    return pl.ds(pl.multiple_of(J * SUPER, SUPER), SUPER)


def _sb_fwd(sbq, kcat, vcat):
    L = sbq.shape[0]
    nq = L // BLOCK

    def body(q_ref, k_ref, v_ref, o_ref):
        i = pl.program_id(1)
        jd = i // GROUP
        q = q_ref[...]
        U_rev, _ = _cumsum_matrices()

        def scores(J, mask=None):
            z, hi, lo, _ = _sb_logits(q, k_ref[_super_rows(J), :], mask)
            return _sb_suffix(z, hi, lo, U_rev)

        def weighted(J, y, later, mask=None):
            w = jnp.exp(y + jnp.concatenate([later] * GROUP, axis=1))
            if mask is not None:
                w = jnp.where(mask, w, 0.0)
            return _dot(w.astype(BF16), v_ref[_super_rows(J), :])

        mask = _sb_diag_mask(i)
        y, tot = scores(jd, mask)
        acc = weighted(jd, y, jnp.zeros((BLOCK, WIDE), F32), mask)
        later = tot
        y, tot = scores(jnp.maximum(jd - 1, 0))

        def step(jj, st):
            acc, later, y, tot = st
            J = jd - 1 - jj
            acc = acc + weighted(J, y, later)
            y_next, tot_next = scores(J - 1)
            return acc, later + tot, y_next, tot_next

        acc, later, y, tot = lax.fori_loop(0, jnp.maximum(jd - 1, 0), step, (acc, later, y, tot))
        last = weighted(0, y, later)
        o_ref[...] = acc + jnp.where(jd >= 1, last, 0.0)

    qspec = pl.BlockSpec((BLOCK, BLOCK), lambda p, i: (i, p))
    kvspec = pl.BlockSpec((kcat.shape[0], BLOCK), lambda p, i: (0, p))
    return pl.pallas_call(
        body, name="sb_fwd", grid=(SB_W // BLOCK, nq),
        in_specs=[qspec, kvspec, kvspec], out_specs=qspec,
        out_shape=jax.ShapeDtypeStruct((L, SB_W), F32),
        compiler_params=_cparams("parallel", "arbitrary"),
    )(sbq, kcat, vcat)


def _sb_bwd(sbq, kcat, vcat, do_sb):
    L = sbq.shape[0]
    nq = L // BLOCK
    n_super = _padded_blocks(L) // GROUP
    LP = _padded_blocks(L) * BLOCK

    def body(q_ref, k_ref, v_ref, do_ref, dq_ref, dk_ref, dv_ref, g_scr, beta_scr):
        i = pl.program_id(1)
        jd = i // GROUP

        @pl.when(i == 0)
        def _():
            dk_ref[...] = jnp.zeros_like(dk_ref)
            dv_ref[...] = jnp.zeros_like(dv_ref)

        lo = _lane_lo((BLOCK, BLOCK))
        q = q_ref[...]
        qf = q.astype(F32)
        do = do_ref[...]
        do_b = do.astype(BF16)
        q_stack = jnp.concatenate([jnp.where(lo, qf, 0.0), jnp.where(lo, 0.0, qf)], axis=0).astype(BF16)
        do_stack = jnp.concatenate([jnp.where(lo, do, 0.0), jnp.where(lo, 0.0, do)], axis=0).astype(BF16)
        U_rev, U_fwd = _cumsum_matrices()
        mask = _sb_diag_mask(i)

        def stack(x):
            return jnp.concatenate([x[:, :BLOCK], x[:, BLOCK:]], axis=0)

        def add_key_rows(ref, J, x, other):
            for t, xt in enumerate(_chunks(x)):
                rows = pl.ds(pl.multiple_of((J * GROUP + t) * BLOCK, BLOCK), BLOCK)
                ref[rows, :] += _dot_tn(stack(xt), other)

        def scores(J, mask=None):
            z, hi, lo, lb = _sb_logits(q, k_ref[_super_rows(J), :], mask)
            beta_scr[J] = jnp.exp(lb).astype(BF16)
            return _sb_suffix(z, hi, lo, U_rev)

        def weights(J, y, later, mask=None):
            w = jnp.exp(y + jnp.concatenate([later] * GROUP, axis=1))
            if mask is not None:
                w = jnp.where(mask, w, 0.0)
            g_scr[J] = _dot_nt(do_b, v_ref[_super_rows(J), :]) * w
            add_key_rows(dv_ref, J, w.astype(BF16), do_stack)

        y, tot = scores(jd, mask)
        weights(jd, y, jnp.zeros((BLOCK, WIDE), F32), mask)
        later = tot
        y, tot = scores(jnp.maximum(jd - 1, 0))

        def down(jj, st):
            later, y, tot = st
            J = jd - 1 - jj
            weights(J, y, later)
            y_next, tot_next = scores(J - 1)
            return later + tot, y_next, tot_next

        later, y, tot = lax.fori_loop(0, jnp.maximum(jd - 1, 0), down, (later, y, tot))

        @pl.when(jd >= 1)
        def _():
            weights(0, y, later)

        def prefix(J):
            excl, run = [], None
            for g in _chunks(g_scr[J]):
                upto = _dot(g.astype(BF16), U_fwd)
                e = upto - g
                excl.append(e if run is None else e + run)
                tot = _head_totals(upto, BLOCK - 1)
                run = tot if run is None else run + tot
            return jnp.concatenate(excl, axis=1), run

        def grads(J, excl, earlier, mask=None):
            g = g_scr[J]
            beta = beta_scr[J].astype(F32)
            dz = g * (1.0 - beta) - (excl + jnp.concatenate([earlier] * GROUP, axis=1)) * beta
            if mask is not None:
                dz = jnp.where(mask, dz, 0.0)
            dz = dz.astype(BF16)
            add_key_rows(dk_ref, J, dz, q_stack)
            return _dot(dz, k_ref[_super_rows(J), :])

        excl, tot_g = prefix(0)

        def up(J, st):
            dq, earlier, excl, tot_g = st
            dq = dq + grads(J, excl, earlier)
            excl_next, tot_next = prefix(J + 1)
            return dq, earlier + tot_g, excl_next, tot_next

        zero = jnp.zeros((BLOCK, WIDE), F32)
        dq, earlier, excl, tot_g = lax.fori_loop(0, jd, up, (jnp.zeros((BLOCK, BLOCK), F32), zero, excl, tot_g))
        dq_ref[...] = dq + grads(jd, excl, earlier, mask)

    qspec = pl.BlockSpec((BLOCK, BLOCK), lambda p, i: (i, p))
    catspec = pl.BlockSpec((kcat.shape[0], BLOCK), lambda p, i: (0, p))
    accspec = pl.BlockSpec((LP, BLOCK), lambda p, i: (0, p))
    acc_shape = jax.ShapeDtypeStruct((LP, SB_W), F32)
    return pl.pallas_call(
        body, name="sb_bwd", grid=(SB_W // BLOCK, nq),
        in_specs=[qspec, catspec, catspec, qspec],
        out_specs=[qspec, accspec, accspec], out_shape=[jax.ShapeDtypeStruct((L, SB_W), F32), acc_shape, acc_shape],
        scratch_shapes=[pltpu.VMEM((n_super, BLOCK, SUPER), F32), pltpu.VMEM((n_super, BLOCK, SUPER), BF16)],
        compiler_params=_cparams("parallel", "arbitrary"),
    )(sbq, kcat, vcat, do_sb)


def _swa_mask(i):
    t = lax.broadcasted_iota(jnp.int32, (BLOCK, 2 * BLOCK), 0)
    s = lax.broadcasted_iota(jnp.int32, (BLOCK, 2 * BLOCK), 1)
    diff = BLOCK + t - s
    return (diff >= 0) & (diff < BLOCK) & ((i - 1) * BLOCK + s >= PAD)


def _swa_probs(q, kk, mask, sink):
    sc = jnp.where(mask, _dot_nt(q, kk), -jnp.inf)
    mx = jnp.maximum(jnp.max(sc, axis=1, keepdims=True), sink)
    e = jnp.exp(sc - mx)
    es = jnp.exp(sink - mx)
    inv = 1.0 / (jnp.sum(e, axis=1, keepdims=True) + es)
    return e * inv, es * inv


def _swa_specs(L):
    prev = lambda w: pl.BlockSpec((BLOCK, w), lambda i: (jnp.maximum(i - 1, 0), 0))
    cur = lambda w: pl.BlockSpec((BLOCK, w), lambda i: (i, 0))
    sink = pl.BlockSpec(memory_space=pltpu.SMEM)
    return [cur(SW_W), prev(SB_W), cur(SB_W), prev(SB_W), cur(SB_W), sink]


def _swa_fwd(swq, swk4, swv4, sinks):
    L = swq.shape[0]

    def body(q_ref, kp_ref, kc_ref, vp_ref, vc_ref, sink_ref, o_ref):
        mask = _swa_mask(pl.program_id(0))
        k2 = jnp.concatenate([kp_ref[...], kc_ref[...]], axis=0)
        v2 = jnp.concatenate([vp_ref[...], vc_ref[...]], axis=0)
        for m in range(SW_W // BLOCK):
            g = m // 4
            sl = slice(m * BLOCK, (m + 1) * BLOCK)
            q = q_ref[:, sl]
            acc = jnp.zeros((BLOCK, BLOCK), F32)
            for par in range(2):
                kv = slice((2 * g + par) * BLOCK, (2 * g + par + 1) * BLOCK)
                p, _ = _swa_probs(q, k2[:, kv], mask, sink_ref[0, 2 * m + par])
                acc = acc + _dot(p.astype(BF16), v2[:, kv])
            o_ref[:, sl] = acc

    return pl.pallas_call(
        body, name="swa_fwd", grid=(L // BLOCK,), in_specs=_swa_specs(L),
        out_specs=pl.BlockSpec((BLOCK, SW_W), lambda i: (i, 0)),
        out_shape=jax.ShapeDtypeStruct((L, SW_W), F32),
        compiler_params=_cparams("parallel"),
    )(swq, swk4, swk4, swv4, swv4, sinks)


def _swa_bwd(swq, swk4, swv4, sinks, do_sw):
    L = swq.shape[0]

    def body(q_ref, kp_ref, kc_ref, vp_ref, vc_ref, sink_ref, do_ref, dq_ref, dkp_ref, dkc_ref, dvp_ref, dvc_ref, ds_ref):
        i = pl.program_id(0)

        @pl.when(i == 0)
        def _():
            ds_ref[...] = jnp.zeros_like(ds_ref)

        mask = _swa_mask(i)
        k2 = jnp.concatenate([kp_ref[...], kc_ref[...]], axis=0)
        v2 = jnp.concatenate([vp_ref[...], vc_ref[...]], axis=0)
        lane = lax.broadcasted_iota(jnp.int32, (8, BLOCK), 1)
        dsink = jnp.zeros((8, BLOCK), F32)
        for g in range(2):
            dkk = [jnp.zeros((2 * BLOCK, BLOCK), F32) for _ in range(2)]
            dvv = [jnp.zeros((2 * BLOCK, BLOCK), F32) for _ in range(2)]
            for m in range(4 * g, 4 * g + 4):
                sl = slice(m * BLOCK, (m + 1) * BLOCK)
                q = q_ref[:, sl]
                do_b = do_ref[:, sl].astype(BF16)
                dq = jnp.zeros((BLOCK, BLOCK), F32)
                for par in range(2):
                    kv = slice((2 * g + par) * BLOCK, (2 * g + par + 1) * BLOCK)
                    kk = k2[:, kv]
                    p, ps = _swa_probs(q, kk, mask, sink_ref[0, 2 * m + par])
                    dp = _dot_nt(do_b, v2[:, kv])
                    dd = jnp.sum(p * dp, axis=1, keepdims=True)
                    ds = (p * (dp - dd)).astype(BF16)
                    dsink = dsink - jnp.where(lane == 2 * m + par, jnp.sum(ps * dd), 0.0)
                    dq = dq + _dot(ds, kk)
                    dkk[par] = dkk[par] + _dot_tn(ds, q)
                    dvv[par] = dvv[par] + _dot_tn(p.astype(BF16), do_b)
                dq_ref[:, sl] = dq
            for par in range(2):
                kv = slice((2 * g + par) * BLOCK, (2 * g + par + 1) * BLOCK)
                dkp_ref[:, kv] = dkk[par][0:BLOCK]
                dkc_ref[:, kv] = dkk[par][BLOCK:2 * BLOCK]
                dvp_ref[:, kv] = dvv[par][0:BLOCK]
                dvc_ref[:, kv] = dvv[par][BLOCK:2 * BLOCK]
        ds_ref[...] += dsink

    part = pl.BlockSpec((BLOCK, SB_W), lambda i: (i, 0))
    row = pl.BlockSpec((BLOCK, SW_W), lambda i: (i, 0))
    pshape = jax.ShapeDtypeStruct((L, SB_W), F32)
    return pl.pallas_call(
        body, name="swa_bwd", grid=(L // BLOCK,), in_specs=_swa_specs(L) + [row],
        out_specs=[row, part, part, part, part, pl.BlockSpec((8, BLOCK), lambda i: (0, 0))],
        out_shape=[jax.ShapeDtypeStruct((L, SW_W), F32), pshape, pshape, pshape, pshape,
                   jax.ShapeDtypeStruct((8, BLOCK), F32)],
        compiler_params=_cparams("arbitrary"),
    )(swq, swk4, swk4, swv4, swv4, sinks, do_sw)


def _gate_specs(tm):
    return [pl.BlockSpec((tm, SW_W), lambda i: (i, 0)), pl.BlockSpec((tm, D_MODEL), lambda i: (i, 1)),
            pl.BlockSpec((tm, D_MODEL), lambda i: (i, 2)), pl.BlockSpec((tm, SB_W), lambda i: (i, 6))]


def _post_fwd(o_sb, o_sw, gate, h0, w_bsb, w_bsw, w_out):
    L = h0.shape[0]
    tm = _row_tile(L, 320)

    def body(osb, osw, swz, gsb, gsw, sbz, h0_ref, wb1, wb2, wo, h1, usb, usw, mrg, ysb, ysw):
        z1 = sbz[...]
        z2 = swz[...]
        u1 = (osb[...] * (z1 * jax.nn.sigmoid(z1))).astype(BF16)
        u2 = (osw[...] * (z2 * jax.nn.sigmoid(z2))).astype(BF16)
        y1 = _dot(u1, wb1[...])
        y2 = _dot(u2, wb2[...])
        merged = (jax.nn.sigmoid(gsb[...]) * y1 + jax.nn.sigmoid(gsw[...]) * y2).astype(BF16)
        h1[...] = h0_ref[...] + _dot(merged, wo[...])
        usb[...] = u1
        usw[...] = u2
        mrg[...] = merged
        ysb[...] = y1
        ysw[...] = y2

    row = lambda w: pl.BlockSpec((tm, w), lambda i: (i, 0))
    whole = lambda a: pl.BlockSpec(a.shape, lambda i: (0, 0))
    shp = lambda w, dt: jax.ShapeDtypeStruct((L, w), dt)
    return pl.pallas_call(
        body, name="post_fwd", grid=(L // tm,),
        in_specs=[row(SB_W), row(SW_W)] + _gate_specs(tm) + [row(D_MODEL), whole(w_bsb), whole(w_bsw), whole(w_out)],
        out_specs=[row(D_MODEL), row(SB_W), row(SW_W), row(D_MODEL), row(D_MODEL), row(D_MODEL)],
        out_shape=[shp(D_MODEL, F32), shp(SB_W, BF16), shp(SW_W, BF16), shp(D_MODEL, BF16),
                   shp(D_MODEL, F32), shp(D_MODEL, F32)],
        compiler_params=_cparams("parallel"),
    )(o_sb, o_sw, gate, gate, gate, gate, h0, w_bsb, w_bsw, w_out)


def _loss_bwd(h1, target, gain2):
    L = h1.shape[0]

    def body(h_ref, t_ref, g_ref, dh_ref, loss_ref, dg_ref):
        i = pl.program_id(0)

        @pl.when(i == 0)
        def _():
            dh_ref[...] = jnp.zeros_like(dh_ref)
            loss_ref[...] = jnp.zeros_like(loss_ref)
            dg_ref[...] = jnp.zeros_like(dg_ref)

        @pl.when(i > 0)
        def _():
            h = h_ref[...]
            g = g_ref[...]
            r = lax.rsqrt(jnp.mean(h * h, axis=-1, keepdims=True) + RMS_EPS)
            n = h * r
            err = n * g - t_ref[...]
            loss_ref[...] += 0.5 * jnp.sum(jnp.mean(err * err, axis=-1, keepdims=True))
            dy = err * (1.0 / D_MODEL)
            dg_ref[...] += jnp.sum(dy * n, axis=0, keepdims=True)
            dn = dy * g
            dh_ref[...] = r * (dn - n * jnp.mean(dn * n, axis=-1, keepdims=True))

    return pl.pallas_call(
        body, name="loss_bwd", grid=(L // BLOCK,),
        in_specs=[pl.BlockSpec((BLOCK, D_MODEL), lambda i: (i, 0)),
                  pl.BlockSpec((BLOCK, D_MODEL), lambda i: (jnp.maximum(i - 1, 0), 0)),
                  pl.BlockSpec((1, D_MODEL), lambda i: (0, 0))],
        out_specs=[pl.BlockSpec((BLOCK, D_MODEL), lambda i: (i, 0)), pl.BlockSpec((1, BLOCK), lambda i: (0, 0)),
                   pl.BlockSpec((1, D_MODEL), lambda i: (0, 0))],
        out_shape=[jax.ShapeDtypeStruct((L, D_MODEL), F32), jax.ShapeDtypeStruct((1, BLOCK), F32),
                   jax.ShapeDtypeStruct((1, D_MODEL), F32)],
        compiler_params=_cparams("arbitrary"),
    )(h1, target, gain2)


def _post_bwd(dh1, gate, y_sb, y_sw, o_sb, o_sw, wt_out, wt_bsb, wt_bsw):
    L = dh1.shape[0]
    tm = _row_tile(L, 320)

    def body(dh, swz, gsb, gsw, sbz, ysb, ysw, osb, osw, wo, wb1, wb2, dy1_ref, dy2_ref, do1_ref, do2_ref, dg_ref):
        dm = _dot(dh[...].astype(BF16), wo[...])
        s1 = jax.nn.sigmoid(gsb[...])
        s2 = jax.nn.sigmoid(gsw[...])
        dy1 = (dm * s1).astype(BF16)
        dy2 = (dm * s2).astype(BF16)
        dy1_ref[...] = dy1
        dy2_ref[...] = dy2
        du1 = _dot(dy1, wb1[...])
        du2 = _dot(dy2, wb2[...])
        z1 = sbz[...]
        z2 = swz[...]
        sz1 = jax.nn.sigmoid(z1)
        sz2 = jax.nn.sigmoid(z2)
        do1_ref[...] = du1 * (z1 * sz1)
        do2_ref[...] = du2 * (z2 * sz2)
        dg_ref[:, 0:SW_W] = (du2 * osw[...] * (sz2 * (1.0 + z2 * (1.0 - sz2)))).astype(BF16)
        dg_ref[:, SW_W:SW_W + D_MODEL] = (dm * ysb[...] * (s1 * (1.0 - s1))).astype(BF16)
        dg_ref[:, SW_W + D_MODEL:SW_W + 2 * D_MODEL] = (dm * ysw[...] * (s2 * (1.0 - s2))).astype(BF16)
        dg_ref[:, SW_W + 2 * D_MODEL:GATE_W] = (du1 * osb[...] * (sz1 * (1.0 + z1 * (1.0 - sz1)))).astype(BF16)

    row = lambda w: pl.BlockSpec((tm, w), lambda i: (i, 0))
    whole = lambda a: pl.BlockSpec(a.shape, lambda i: (0, 0))
    shp = lambda w, dt: jax.ShapeDtypeStruct((L, w), dt)
    return pl.pallas_call(
        body, name="post_bwd", grid=(L // tm,),
        in_specs=[row(D_MODEL)] + _gate_specs(tm) + [row(D_MODEL), row(D_MODEL), row(SB_W), row(SW_W),
                                                     whole(wt_out), whole(wt_bsb), whole(wt_bsw)],
        out_specs=[row(D_MODEL), row(D_MODEL), row(SB_W), row(SW_W), row(GATE_W)],
        out_shape=[shp(D_MODEL, BF16), shp(D_MODEL, BF16), shp(SB_W, F32), shp(SW_W, F32), shp(GATE_W, BF16)],
        compiler_params=_cparams("parallel"),
    )(dh1, gate, gate, gate, gate, y_sb, y_sw, o_sb, o_sw, wt_out, wt_bsb, wt_bsw)


def _assemble_dqkv(dswq, dsbq, dsbk, dsbv, dkp, dkc, dvp, dvc, cos_t, sin_t):
    L = dswq.shape[0]
    tm = BLOCK
    nb = L // BLOCK

    def body(dq_ref, dsq_ref, dsk_ref, dsv_ref, dkp_ref, dkc_ref, dvp_ref, dvc_ref, c_ref, s_ref, o_ref):
        C = c_ref[...]
        S = s_ref[...]
        lane = lax.broadcasted_iota(jnp.int32, (tm, BLOCK), 1)
        lo32 = lane % HEAD_DIM < HEAD_DIM // 2
        lo = lane < HEAD_DIM
        has_next = (pl.program_id(0) + 1 < nb).astype(F32)

        def unrope(dy):
            return dy * C + _rope_partner(dy * S, lo32)

        def fold(cur_ref, next_ref):
            b00, b01, b10, b11 = (cur_ref[:, k * BLOCK:(k + 1) * BLOCK] + has_next * next_ref[:, k * BLOCK:(k + 1) * BLOCK]
                                  for k in range(4))
            return jnp.where(lo, b00 + pltpu.roll(b01, HEAD_DIM, 1), pltpu.roll(b10, HEAD_DIM, 1) + b11)

        for m in range(SW_W // BLOCK):
            sl = slice(m * BLOCK, (m + 1) * BLOCK)
            o_ref[:, sl] = unrope(dq_ref[:, sl] * Q_SCALE).astype(BF16)
        o_ref[:, SW_W:SW_W + SB_W] = (dsq_ref[...] * Q_SCALE).astype(BF16)
        o_ref[:, SW_W + SB_W:SW_W + 2 * SB_W] = dsk_ref[...].astype(BF16)
        o_ref[:, SW_W + 2 * SB_W:SW_W + 3 * SB_W] = dsv_ref[...].astype(BF16)
        base = SW_W + 3 * SB_W
        o_ref[:, base:base + KV_W] = unrope(fold(dkc_ref, dkp_ref)).astype(BF16)
        o_ref[:, base + KV_W:base + 2 * KV_W] = fold(dvc_ref, dvp_ref).astype(BF16)

    row = lambda w: pl.BlockSpec((tm, w), lambda i: (i, 0))
    nxt = pl.BlockSpec((tm, SB_W), lambda i: (jnp.minimum(i + 1, nb - 1), 0))
    return pl.pallas_call(
        body, name="assemble_dqkv", grid=(nb,),
        in_specs=[row(SW_W), row(SB_W), row(SB_W), row(SB_W), nxt, row(SB_W), nxt, row(SB_W), row(BLOCK), row(BLOCK)],
        out_specs=row(QKV_W), out_shape=jax.ShapeDtypeStruct((L, QKV_W), BF16),
        compiler_params=_cparams("parallel"),
    )(dswq, dsbq, dsbk, dsbv, dkp, dkc, dvp, dvc, cos_t, sin_t)


def _norm_bwd(dxn, h0, dh1, gain):
    L = h0.shape[0]

    def body(dx_ref, h_ref, dh_ref, g_ref, gx_ref, dm_ref, dg_ref):
        i = pl.program_id(0)

        @pl.when(i == 0)
        def _():
            dg_ref[...] = jnp.zeros_like(dg_ref)

        h = h_ref[...]
        dxn_t = dx_ref[...]
        r = lax.rsqrt(jnp.mean(h * h, axis=-1, keepdims=True) + RMS_EPS)
        n = h * r
        dg_ref[...] += jnp.sum(dxn_t * n, axis=0, keepdims=True)
        dn = dxn_t * g_ref[...]
        dh0 = dh_ref[...] + r * (dn - n * jnp.mean(dn * n, axis=-1, keepdims=True))
        gx_ref[...] = dh0

        @pl.when(i == 0)
        def _():
            dm_ref[...] = dh0[PAD:BLOCK]

    row = pl.BlockSpec((BLOCK, D_MODEL), lambda i: (i, 0))
    return pl.pallas_call(
        body, name="norm_bwd", grid=(L // BLOCK,),
        in_specs=[row, row, row, pl.BlockSpec((1, D_MODEL), lambda i: (0, 0))],
        out_specs=[pl.BlockSpec((BLOCK, D_MODEL), lambda i: (jnp.maximum(i - 1, 0), 0)),
                   pl.BlockSpec((N_META, D_MODEL), lambda i: (0, 0)), pl.BlockSpec((1, D_MODEL), lambda i: (0, 0))],
        out_shape=[jax.ShapeDtypeStruct((L - BLOCK, D_MODEL), F32), jax.ShapeDtypeStruct((N_META, D_MODEL), F32),
                   jax.ShapeDtypeStruct((1, D_MODEL), F32)],
        compiler_params=_cparams("arbitrary"),
    )(dxn, h0, dh1, gain)


def _adamw(w, g, m, v, name):
    R, C = w.shape
    tr = _row_tile(R, 256) if R % 16 == 0 else R

    def body(w_ref, g_ref, m_ref, v_ref, d_ref, nm_ref, nv_ref):
        g = g_ref[...]
        m_new = ADAM_B1 * m_ref[...] + (1.0 - ADAM_B1) * g
        v_new = ADAM_B2 * v_ref[...] + (1.0 - ADAM_B2) * (g * g)
        m_hat = m_new / (1.0 - ADAM_B1 ** ADAM_STEP)
        v_hat = v_new / (1.0 - ADAM_B2 ** ADAM_STEP)
        d_ref[...] = -ADAM_LR * (m_hat / (jnp.sqrt(v_hat) + ADAM_EPS) + ADAM_WD * w_ref[...])
        nm_ref[...] = m_new
        nv_ref[...] = v_new

    spec = pl.BlockSpec((tr, C), lambda i: (i, 0))
    shp = jax.ShapeDtypeStruct((R, C), F32)
    return pl.pallas_call(
        body, name=name, grid=(R // tr,), in_specs=[spec] * 4, out_specs=[spec] * 3, out_shape=[shp] * 3,
        compiler_params=_cparams("parallel"),
    )(w, g, m, v)


def _sum_leading(a, name):
    n, R, C = a.shape
    tr = _row_tile(R, 160) if R % 16 == 0 else R

    def body(a_ref, o_ref):
        acc = a_ref[0]
        for k in range(1, n):
            acc = acc + a_ref[k]
        o_ref[...] = acc

    return pl.pallas_call(
        body, name=name, grid=(R // tr,), in_specs=[pl.BlockSpec((n, tr, C), lambda i: (0, i, 0))],
        out_specs=pl.BlockSpec((tr, C), lambda i: (i, 0)), out_shape=jax.ShapeDtypeStruct((R, C), a.dtype),
        compiler_params=_cparams("parallel"),
    )(a)


def _place():
    x, y, c = lax.axis_index("x"), lax.axis_index("y"), lax.axis_index("c")
    return x, y, c


def _all_gather(block, name):
    R, C = block.shape

    def body(x_ref, out_ref, send_sems, recv_sems, local_sem):
        x, y, c = _place()
        me, sibling = (x, y, c), (x, y, 1 - c)
        chips = [(1 - x, y), (x, 1 - y), (1 - x, 1 - y)]

        def slot(px, py, pc):
            return out_ref.at[4 * px + 2 * py + pc]

        def copy(k, blk, to, src=None):
            return pltpu.make_async_remote_copy(
                src_ref=slot(*blk) if src is None else src, dst_ref=slot(*blk),
                send_sem=send_sems.at[k], recv_sem=recv_sems.at[k], device_id=to, device_id_type=MESH)

        mine = pltpu.make_async_copy(x_ref, slot(*me), local_sem)
        mine.start()
        first = [copy(0, me, sibling, src=x_ref)]
        first += [copy(1 + j, me, (*chip, c), src=x_ref) for j, chip in enumerate(chips)]
        for cp in first:
            cp.start()
        passed = [copy(4 + j, (*chip, c), sibling) for j, chip in enumerate(chips)]
        for j, chip in enumerate(chips):
            copy(1 + j, (*chip, c), me).wait_recv()
            passed[j].start()
        copy(0, sibling, me).wait_recv()
        for j, chip in enumerate(chips):
            copy(4 + j, (*chip, 1 - c), me).wait_recv()
        for cp in first + passed:
            cp.wait_send()
        mine.wait()

    return pl.pallas_call(
        body, name=name, in_specs=[ANY], out_specs=ANY,
        out_shape=jax.ShapeDtypeStruct((N_DEV, R, C), block.dtype),
        scratch_shapes=[pltpu.SemaphoreType.DMA((7,)), pltpu.SemaphoreType.DMA((7,)), pltpu.SemaphoreType.DMA],
    )(block)


def _pair_exchange(g8):
    _, R, C = g8.shape

    def body(g_ref, out_ref, send_sems, recv_sems):
        x, y, c = _place()
        sibling = (x, y, 1 - c)
        sends = [pltpu.make_async_remote_copy(
            src_ref=g_ref.at[2 * s + (1 - c)], dst_ref=out_ref.at[s], send_sem=send_sems.at[s],
            recv_sem=recv_sems.at[s], device_id=sibling, device_id_type=MESH) for s in range(N_CHIP)]
        for cp in sends:
            cp.start()
        for s in range(N_CHIP):
            pltpu.make_async_remote_copy(
                src_ref=g_ref.at[s], dst_ref=out_ref.at[s], send_sem=send_sems.at[s],
                recv_sem=recv_sems.at[s], device_id=sibling, device_id_type=MESH).wait_recv()
        for cp in sends:
            cp.wait_send()

    return pl.pallas_call(
        body, name="grad_pair_exchange", in_specs=[ANY], out_specs=ANY,
        out_shape=jax.ShapeDtypeStruct((N_CHIP, R, C), g8.dtype),
        scratch_shapes=[pltpu.SemaphoreType.DMA((N_CHIP,)), pltpu.SemaphoreType.DMA((N_CHIP,))],
    )(g8)


def _add_selected(a, b, a_index, name):
    select, sel_fn = a_index
    bs = b if isinstance(b, (list, tuple)) else [b]
    m, R, C = bs[0].shape
    tr = _row_tile(R, 160)

    def body(sel_ref, a_ref, *refs):
        acc = a_ref[...]
        for r in refs[:-1]:
            acc = acc + r[...]
        refs[-1][...] = acc

    blk = lambda fn: pl.BlockSpec((None, tr, C), fn)
    return pl.pallas_call(
        body, name=name,
        grid_spec=pltpu.PrefetchScalarGridSpec(
            num_scalar_prefetch=1, grid=(m, R // tr),
            in_specs=[blk(lambda s, i, sel: (sel_fn(s, sel[0]), i, 0))] + [blk(lambda s, i, sel: (s, i, 0))] * len(bs),
            out_specs=blk(lambda s, i, sel: (s, i, 0))),
        out_shape=jax.ShapeDtypeStruct((m, R, C), a.dtype),
        compiler_params=_cparams("parallel", "parallel"),
    )(jnp.reshape(select, (1,)).astype(jnp.int32), a, *bs)


def _chip_scatter(p4):
    _, R, C = p4.shape

    def body(p_ref, out_x, out_y, out_xy, send_sems, recv_sems):
        x, y, c = _place()
        chips = [(1 - x, y), (x, 1 - y), (1 - x, 1 - y)]
        outs = [out_x, out_y, out_xy]
        sends = [pltpu.make_async_remote_copy(
            src_ref=p_ref.at[2 * cx + cy], dst_ref=outs[j].at[0], send_sem=send_sems.at[j],
            recv_sem=recv_sems.at[j], device_id=(cx, cy, c), device_id_type=MESH) for j, (cx, cy) in enumerate(chips)]
        for cp in sends:
            cp.start()
        for j, (cx, cy) in enumerate(chips):
            pltpu.make_async_remote_copy(
                src_ref=p_ref.at[0], dst_ref=outs[j].at[0], send_sem=send_sems.at[j],
                recv_sem=recv_sems.at[j], device_id=(cx, cy, c), device_id_type=MESH).wait_recv()
        for cp in sends:
            cp.wait_send()

    shp = jax.ShapeDtypeStruct((1, R, C), p4.dtype)
    return pl.pallas_call(
        body, name="grad_chip_scatter", in_specs=[ANY], out_specs=[ANY, ANY, ANY], out_shape=[shp, shp, shp],
        scratch_shapes=[pltpu.SemaphoreType.DMA((3,)), pltpu.SemaphoreType.DMA((3,))],
    )(p4)


def _pair_share(r):
    R, C = r.shape

    def body(r_ref, out_ref, send_sem, recv_sem):
        x, y, c = _place()
        copy = pltpu.make_async_remote_copy(src_ref=r_ref, dst_ref=out_ref, send_sem=send_sem, recv_sem=recv_sem,
                                            device_id=(x, y, 1 - c), device_id_type=MESH)
        copy.start()
        copy.wait()

    return pl.pallas_call(
        body, name="grad_pair_share", in_specs=[ANY], out_specs=ANY,
        out_shape=jax.ShapeDtypeStruct((R, C), r.dtype),
        scratch_shapes=[pltpu.SemaphoreType.DMA, pltpu.SemaphoreType.DMA],
    )(r)


def _gather_direct(block, name):
    R, C = block.shape

    def body(x_ref, out_ref, send_sems, recv_sems, local_sem):
        x, y, c = _place()
        me = 4 * x + 2 * y + c
        local = pltpu.make_async_copy(x_ref, out_ref.at[me], local_sem)
        local.start()
        peers = []
        for k in range(1, N_DEV):
            px, py, pc = x ^ (k >> 2), y ^ ((k >> 1) & 1), c ^ (k & 1)
            peers.append((k, (px, py, pc), 4 * px + 2 * py + pc))
        sends = [pltpu.make_async_remote_copy(
            src_ref=x_ref, dst_ref=out_ref.at[me], send_sem=send_sems.at[k - 1], recv_sem=recv_sems.at[k - 1],
            device_id=dev, device_id_type=MESH) for k, dev, _ in peers]
        for cp in sends:
            cp.start()
        for k, dev, idx in peers:
            pltpu.make_async_remote_copy(
                src_ref=x_ref, dst_ref=out_ref.at[idx], send_sem=send_sems.at[k - 1], recv_sem=recv_sems.at[k - 1],
                device_id=dev, device_id_type=MESH).wait_recv()
        for cp in sends:
            cp.wait_send()
        local.wait()

    return pl.pallas_call(
        body, name=name, in_specs=[ANY], out_specs=ANY,
        out_shape=jax.ShapeDtypeStruct((N_DEV, R, C), block.dtype),
        scratch_shapes=[pltpu.SemaphoreType.DMA((7,)), pltpu.SemaphoreType.DMA((7,)), pltpu.SemaphoreType.DMA],
    )(block)


W_IN_SHARD = IN_COLS // N_CHIP
ROWS_W_IN = (D_MODEL // 2) * W_IN_SHARD // D_MODEL
ROWS_BSB = (SB_W // 2) * (D_MODEL // N_CHIP) // D_MODEL
ROWS_SQ = D_MODEL // N_CHIP // 2
ROWS_BIG = ROWS_W_IN + ROWS_BSB + 2 * ROWS_SQ


def _half(a, c, axis):
    n = a.shape[axis] // 2
    return lax.dynamic_slice_in_dim(a, c * n, n, axis)


def _pack_shard_half(w_in, w_bsb, w_bsw, w_out, c):
    parts = [_half(w_in, c, 0).reshape(ROWS_W_IN, D_MODEL), _half(w_bsb, c, 0).reshape(ROWS_BSB, D_MODEL),
             _half(w_bsw, c, 0), _half(w_out, c, 0)]
    return jnp.concatenate(parts, axis=0)


def _unpack_full(blocks):
    b = blocks.reshape(N_CHIP, 2, ROWS_BIG, D_MODEL)
    o = 0
    w_in = b[:, :, o:o + ROWS_W_IN].reshape(N_CHIP, D_MODEL, W_IN_SHARD)
    o += ROWS_W_IN
    w_bsb = b[:, :, o:o + ROWS_BSB].reshape(N_CHIP, SB_W, D_MODEL // N_CHIP)
    o += ROWS_BSB
    w_bsw = b[:, :, o:o + ROWS_SQ].reshape(D_MODEL, D_MODEL)
    o += ROWS_SQ
    w_out = b[:, :, o:o + ROWS_SQ].reshape(D_MODEL, D_MODEL)
    w_in = jnp.transpose(w_in, (1, 0, 2)).reshape(D_MODEL, IN_COLS)
    w_bsb = jnp.transpose(w_bsb, (1, 0, 2)).reshape(SB_W, D_MODEL)
    return w_in, w_bsb, w_bsw, w_out


def _pack_grads(dw_in, dw_bsb, dw_bsw, dw_out):
    a = jnp.transpose(dw_in.reshape(2, D_MODEL // 2, N_CHIP, W_IN_SHARD), (2, 0, 1, 3)).reshape(N_CHIP, 2, ROWS_W_IN, D_MODEL)
    b = jnp.transpose(dw_bsb.reshape(2, SB_W // 2, N_CHIP, D_MODEL // N_CHIP), (2, 0, 1, 3)).reshape(N_CHIP, 2, ROWS_BSB, D_MODEL)
    c = dw_bsw.reshape(N_CHIP, 2, ROWS_SQ, D_MODEL)
    d = dw_out.reshape(N_CHIP, 2, ROWS_SQ, D_MODEL)
    return jnp.concatenate([a, b, c, d], axis=2).reshape(N_DEV, ROWS_BIG, D_MODEL)


def _unpack_shard(full2):
    o = 0
    w_in = full2[:, o:o + ROWS_W_IN].reshape(D_MODEL, W_IN_SHARD)
    o += ROWS_W_IN
    w_bsb = full2[:, o:o + ROWS_BSB].reshape(SB_W, D_MODEL // N_CHIP)
    o += ROWS_BSB
    w_bsw = full2[:, o:o + ROWS_SQ].reshape(D_MODEL // N_CHIP, D_MODEL)
    o += ROWS_SQ
    w_out = full2[:, o:o + ROWS_SQ].reshape(D_MODEL // N_CHIP, D_MODEL)
    return w_in, w_bsb, w_bsw, w_out


_QKV_FROM_IN = ((1536, 2560), (0, 512), (512, 1024), (1024, 1536), (2560, 2688), (2688, 2816))
_GATE_FROM_IN = ((3328, 4352), (4352, 5376), (5376, 6400), (2816, 3328))


def _split_w_in(w_in):
    qkv = jnp.concatenate([w_in[:, a:b] for a, b in _QKV_FROM_IN], axis=1)
    gate = jnp.concatenate([w_in[:, a:b] for a, b in _GATE_FROM_IN], axis=1)
    return qkv, gate


def _join_dw_in(dqkv, dgate):
    q = lambda a, b: dqkv[:, a:b]
    g = lambda a, b: dgate[:, a:b]
    return jnp.concatenate([q(1024, 1536), q(1536, 2048), q(2048, 2560), q(0, 1024), q(2560, 2688), q(2688, 2816),
                            g(3072, 3584), g(0, 1024), g(1024, 2048), g(2048, 3072)], axis=1)


def _rope_tables(L):
    half = HEAD_DIM // 2
    inv = ROPE_THETA ** (-jnp.arange(half, dtype=F32) / half)
    pos = (jnp.arange(L) - PAD).astype(F32)
    ang = pos[:, None] * inv[None, :]
    cos, sin = jnp.cos(ang), jnp.sin(ang)
    return jnp.tile(jnp.concatenate([cos, cos], axis=1), (1, 2)), jnp.tile(jnp.concatenate([-sin, sin], axis=1), (1, 2))


def kernel(x, meta_tokens, norm_gain, w_in, w_branch_sb, w_branch_swa, w_out, attn_sinks, final_norm_gain, loss_target, m_meta_tokens, m_norm_gain, m_w_in, m_w_branch_sb, m_w_branch_swa, m_w_out, m_attn_sinks, m_final_norm_gain, v_meta_tokens, v_norm_gain, v_w_in, v_w_branch_sb, v_w_branch_swa, v_w_out, v_attn_sinks, v_final_norm_gain):
    xi, yi, ci = _place()
    chip = 2 * xi + yi
    seq = x.shape[1]
    L = seq + BLOCK

    wpack = _pack_shard_half(w_in[0], w_branch_sb[0], w_branch_swa[0], w_out[0], ci).astype(BF16)
    w_in_f, w_bsb_f, w_bsw_f, w_out_f = _unpack_full(_all_gather(wpack, "weight_all_gather"))
    mg = _all_gather(_half(meta_tokens, ci, 0), "meta_all_gather").reshape(N_CHIP, 2, N_META // 2, D_MODEL // N_CHIP)
    meta_full = jnp.transpose(mg, (1, 2, 0, 3)).reshape(N_META, D_MODEL)
    w_qkv, w_gate = _split_w_in(w_in_f)

    h0 = jnp.concatenate([jnp.zeros((PAD, D_MODEL), F32), meta_full, x[0]], axis=0)
    cos_t, sin_t = _rope_tables(L)
    xn = _norm_fwd(h0, norm_gain)
    qkv = _mm([(xn, w_qkv)], F32, "in_proj_qkv", tn_pref=1408)
    gate = _mm([(xn, w_gate)], F32, "in_proj_gate")
    swq, sbq, kcat, vcat, swk4, swv4 = _prep_qkv(qkv, cos_t, sin_t)
    o_sb = _sb_fwd(sbq, kcat, vcat)
    o_sw = _swa_fwd(swq, swk4, swv4, attn_sinks)
    h1, u_sb, u_sw, merged, y_sb, y_sw = _post_fwd(o_sb, o_sw, gate, h0, w_bsb_f, w_bsw_f, w_out_f)

    dh1, loss_part, dgain2 = _loss_bwd(h1, loss_target[0], final_norm_gain.reshape(1, D_MODEL))
    dy_sb, dy_sw, do_sb, do_sw, dgate = _post_bwd(dh1, gate, y_sb, y_sw, o_sb, o_sw, w_out_f.T, w_bsb_f.T, w_bsw_f.T)
    dsbq, dsbk, dsbv = _sb_bwd(sbq, kcat, vcat, do_sb)
    dswq, dkp, dkc, dvp, dvc, dsink = _swa_bwd(swq, swk4, swv4, attn_sinks, do_sw)
    dqkv = _assemble_dqkv(dswq, dsbq, dsbk, dsbv, dkp, dkc, dvp, dvc, cos_t, sin_t)
    dxn = _mm([(dqkv, w_qkv.T), (dgate, w_gate.T)], F32, "dxn", tm_pref=320, tn_pref=1024)
    grad_x, dmeta, dgain1 = _norm_bwd(dxn, h0, dh1, norm_gain)
    dw_qkv = _mm_tn(xn, dqkv, "dw_qkv", tn_pref=1408)
    dw_gate = _mm_tn(xn, dgate, "dw_gate")
    dw_bsb = _mm_tn(u_sb, dy_sb, "dw_branch_sb", tn_pref=1024)
    dw_bsw = _mm_tn(u_sw, dy_sw, "dw_branch_swa", tn_pref=1024)
    dw_out = _mm_tn(merged, dh1, "dw_out", tn_pref=1024)

    g8 = _pack_grads(_join_dw_in(dw_qkv, dw_gate), dw_bsb, dw_bsw, dw_out)
    p4 = _add_selected(g8, _pair_exchange(g8), (ci, lambda s, c: 2 * s + c), "grad_pair_sum")
    r_half = _add_selected(p4, list(_chip_scatter(p4)), (chip, lambda s, own: own), "grad_chip_sum")[0]
    r_sib = _pair_share(r_half)
    full2 = jnp.where(ci == 0, jnp.stack([r_half, r_sib]), jnp.stack([r_sib, r_half]))
    g_in, g_bsb, g_bsw, g_out = _unpack_shard(full2)

    small = jnp.concatenate([dmeta, dgain1, dgain2, jnp.pad(dsink[0:1], ((0, 0), (0, D_MODEL - BLOCK))),
                             jnp.pad(loss_part, ((0, 0), (0, D_MODEL - BLOCK))),
                             jnp.zeros((32 - N_META - 4, D_MODEL), F32)], axis=0)
    tot = _sum_leading(_gather_direct(small, "small_all_gather"), "small_sum")
    g_meta = lax.dynamic_slice_in_dim(tot[0:N_META], chip * (D_MODEL // N_CHIP), D_MODEL // N_CHIP, 1)
    g_gain1 = tot[N_META:N_META + 1]
    g_gain2 = tot[N_META + 1]
    g_sinks = tot[N_META + 2:N_META + 3, 0:16]
    loss = tot[N_META + 3, 0]

    def upd(w, g, m, v, name):
        shape = w.shape
        as2d = lambda a: a.reshape(-1, shape[-1])
        d, nm, nv = _adamw(as2d(w), as2d(g), as2d(m), as2d(v), name)
        return g.reshape(shape), d.reshape(shape), nm.reshape(shape), nv.reshape(shape)

    res = [
        upd(meta_tokens, g_meta, m_meta_tokens, v_meta_tokens, "adamw_meta"),
        upd(norm_gain, g_gain1, m_norm_gain, v_norm_gain, "adamw_norm_gain"),
        upd(w_in, g_in, m_w_in, v_w_in, "adamw_w_in"),
        upd(w_branch_sb, g_bsb, m_w_branch_sb, v_w_branch_sb, "adamw_w_branch_sb"),
        upd(w_branch_swa, g_bsw, m_w_branch_swa, v_w_branch_swa, "adamw_w_branch_swa"),
        upd(w_out, g_out, m_w_out, v_w_out, "adamw_w_out"),
        upd(attn_sinks, g_sinks, m_attn_sinks, v_attn_sinks, "adamw_attn_sinks"),
        upd(final_norm_gain, g_gain2, m_final_norm_gain, v_final_norm_gain, "adamw_final_norm_gain"),
    ]
    grads, deltas, new_m, new_v = zip(*res)
    return (loss, grad_x.reshape(1, seq, D_MODEL), *grads, *deltas, *new_m, *new_v)
```

```python
import functools

import jax
import jax.numpy as jnp
from jax import lax
from jax.experimental import pallas as pl
from jax.experimental.pallas import tpu as pltpu

F32 = jnp.float32
BF16 = jnp.bfloat16

D_MODEL = 1024
BLOCK = 128
N_META = 16
PAD = BLOCK - N_META
HEAD_DIM = 64
SB_W = 512
SW_W = 1024
KV_W = 128
QKV_W = SW_W + 3 * SB_W + 2 * KV_W
GATE_W = 2 * D_MODEL + SW_W + SB_W
IN_COLS = QKV_W + GATE_W
ROPE_THETA = 10000.0
RMS_EPS = 1e-6
Q_SCALE = HEAD_DIM ** -0.5

ADAM_LR = 0.001
ADAM_B1 = 0.9
ADAM_B2 = 0.999
ADAM_EPS = 1e-08
ADAM_WD = 0.01
ADAM_STEP = 10

N_DEV = 8
N_CHIP = 4
MESH = pl.DeviceIdType.MESH
VMEM_LIMIT = 56 * 1024 * 1024
ANY = pl.BlockSpec(memory_space=pl.ANY)


def _cparams(*sem):
    return pltpu.CompilerParams(dimension_semantics=sem, vmem_limit_bytes=VMEM_LIMIT)


def _row_tile(n, pref):
    best = None
    for t in range(16, pref + 1, 16):
        if n % t == 0:
            best = t
    assert best is not None, (n, pref)
    return best


def _col_tile(n, pref):
    best = None
    for t in range(128, pref + 1, 128):
        if n % t == 0:
            best = t
    assert best is not None, (n, pref)
    return best


def _dot(a, b):
    return jnp.dot(a, b, preferred_element_type=F32)


def _dot_nt(a, b):
    return lax.dot_general(a, b, (((1,), (1,)), ((), ())), preferred_element_type=F32)


def _dot_tn(a, b):
    return lax.dot_general(a, b, (((0,), (0,)), ((), ())), preferred_element_type=F32)


def _lane_lo(shape):
    return lax.broadcasted_iota(jnp.int32, shape, len(shape) - 1) % BLOCK < HEAD_DIM


def _mm(pairs, out_dtype, name, tm_pref=640, tn_pref=1792):
    M = pairs[0][0].shape[0]
    N = pairs[0][1].shape[1]
    tm = _row_tile(M, tm_pref)
    tn = _col_tile(N, tn_pref)
    n_pairs = len(pairs)

    def body(*refs):
        o_ref = refs[-1]
        acc = None
        for p in range(n_pairs):
            d = _dot(refs[2 * p][...].astype(BF16), refs[2 * p + 1][...])
            acc = d if acc is None else acc + d
        o_ref[...] = acc.astype(out_dtype)

    in_specs, args = [], []
    for a, b in pairs:
        k = a.shape[1]
        in_specs += [pl.BlockSpec((tm, k), lambda n, m: (m, 0)), pl.BlockSpec((k, tn), lambda n, m: (0, n))]
        args += [a, b]
    return pl.pallas_call(
        body, name=name, grid=(N // tn, M // tm), in_specs=in_specs,
        out_specs=pl.BlockSpec((tm, tn), lambda n, m: (m, n)),
        out_shape=jax.ShapeDtypeStruct((M, N), out_dtype),
        compiler_params=_cparams("parallel", "arbitrary"),
    )(*args)


def _mm_tn(a, b, name, tn_pref=1792, tl_pref=640):
    L, M = a.shape
    N = b.shape[1]
    tn = _col_tile(N, tn_pref)
    tl = _row_tile(L, tl_pref)

    def body(a_ref, b_ref, o_ref):
        @pl.when(pl.program_id(1) == 0)
        def _():
            o_ref[...] = jnp.zeros_like(o_ref)
        o_ref[...] += _dot_tn(a_ref[...].astype(BF16), b_ref[...].astype(BF16))

    return pl.pallas_call(
        body, name=name, grid=(N // tn, L // tl),
        in_specs=[pl.BlockSpec((tl, M), lambda n, l: (l, 0)), pl.BlockSpec((tl, tn), lambda n, l: (l, n))],
        out_specs=pl.BlockSpec((M, tn), lambda n, l: (0, n)),
        out_shape=jax.ShapeDtypeStruct((M, N), F32),
        compiler_params=_cparams("parallel", "arbitrary"),
    )(a, b)


def _norm_fwd(h0, gain):
    L = h0.shape[0]
    tm = _row_tile(L, 640)

    def body(h_ref, g_ref, o_ref):
        h = h_ref[...]
        r = lax.rsqrt(jnp.mean(h * h, axis=-1, keepdims=True) + RMS_EPS)
        o_ref[...] = ((h * r) * g_ref[...]).astype(BF16)

    return pl.pallas_call(
        body, name="norm_fwd", grid=(L // tm,),
        in_specs=[pl.BlockSpec((tm, D_MODEL), lambda i: (i, 0)), pl.BlockSpec((1, D_MODEL), lambda i: (0, 0))],
        out_specs=pl.BlockSpec((tm, D_MODEL), lambda i: (i, 0)),
        out_shape=jax.ShapeDtypeStruct((L, D_MODEL), BF16),
        compiler_params=_cparams("parallel"),
    )(h0, gain)


def _rope_partner(x, lo32):
    return jnp.where(lo32, pltpu.roll(x, 96, 1), pltpu.roll(x, 32, 1))


def _prep_qkv(qkv, cos_t, sin_t):
    L = qkv.shape[0]
    tm = BLOCK * max(d for d in range(1, 6) if (L // BLOCK) % d == 0)
    n_steps = L // tm
    assert L // BLOCK + tm // BLOCK >= _padded_blocks(L)

    def body(x_ref, c_ref, s_ref, swq, sbq, kcat, vcat, swk4, swv4):
        C = c_ref[...]
        S = s_ref[...]
        lane = lax.broadcasted_iota(jnp.int32, (tm, BLOCK), 1)
        lo32 = lane % HEAD_DIM < HEAD_DIM // 2
        lo = lane < HEAD_DIM

        def rope(x):
            return x * C + _rope_partner(x, lo32) * S

        for m in range(SW_W // BLOCK):
            sl = slice(m * BLOCK, (m + 1) * BLOCK)
            swq[:, sl] = (rope(x_ref[:, sl]) * Q_SCALE).astype(BF16)
        for m in range(SB_W // BLOCK):
            sl = slice(m * BLOCK, (m + 1) * BLOCK)
            sbq[:, sl] = (x_ref[:, SW_W + m * BLOCK:SW_W + (m + 1) * BLOCK] * Q_SCALE).astype(BF16)
            k = x_ref[:, SW_W + SB_W + m * BLOCK:SW_W + SB_W + (m + 1) * BLOCK]
            v = x_ref[:, SW_W + 2 * SB_W + m * BLOCK:SW_W + 2 * SB_W + (m + 1) * BLOCK]
            for src, dst in ((k, kcat), (v, vcat)):
                even = jnp.where(lo, src, 0.0).astype(BF16)
                odd = jnp.where(lo, 0.0, src).astype(BF16)
                for b in range(tm // BLOCK):
                    dst[2 * b * BLOCK:(2 * b + 1) * BLOCK, sl] = even[b * BLOCK:(b + 1) * BLOCK]
                    dst[(2 * b + 1) * BLOCK:(2 * b + 2) * BLOCK, sl] = odd[b * BLOCK:(b + 1) * BLOCK]
        base = SW_W + 3 * SB_W
        kx = rope(x_ref[:, base:base + KV_W])
        vx = x_ref[:, base + KV_W:base + 2 * KV_W]
        for src, dst in ((kx, swk4), (vx, swv4)):
            sw = pltpu.roll(src, HEAD_DIM, 1)
            dst[:, 0:128] = jnp.where(lo, src, 0.0).astype(BF16)
            dst[:, 128:256] = jnp.where(lo, 0.0, sw).astype(BF16)
            dst[:, 256:384] = jnp.where(lo, sw, 0.0).astype(BF16)
            dst[:, 384:512] = jnp.where(lo, 0.0, src).astype(BF16)

        @pl.when(pl.program_id(0) == n_steps)
        def _():
            kcat[...] = jnp.zeros_like(kcat)
            vcat[...] = jnp.zeros_like(vcat)

    row = lambda w: pl.BlockSpec((tm, w), lambda i: (jnp.minimum(i, n_steps - 1), 0))
    row2 = pl.BlockSpec((2 * tm, SB_W), lambda i: (i, 0))
    shp = lambda r, w: jax.ShapeDtypeStruct((r, w), BF16)
    return pl.pallas_call(
        body, name="prep_qkv", grid=(n_steps + 1,),
        in_specs=[row(QKV_W), row(BLOCK), row(BLOCK)],
        out_specs=[row(SW_W), row(SB_W), row2, row2, row(SB_W), row(SB_W)],
        out_shape=[shp(L, SW_W), shp(L, SB_W), shp(2 * (L + tm), SB_W), shp(2 * (L + tm), SB_W), shp(L, SB_W),
                   shp(L, SB_W)],
        compiler_params=_cparams("arbitrary"),
    )(qkv, cos_t, sin_t)


WIDE = 2 * BLOCK
GROUP = 4
SUPER = GROUP * WIDE
QROWS = 2 * BLOCK


def _padded_blocks(L):
    return -(-(L // BLOCK) // GROUP) * GROUP


def _cumsum_matrices():
    r = lax.broadcasted_iota(jnp.int32, (WIDE, WIDE), 0)
    c = lax.broadcasted_iota(jnp.int32, (WIDE, WIDE), 1)
    same = (r < BLOCK) == (c < BLOCK)
    rev = jnp.where(same & (r >= c), 1.0, 0.0).astype(BF16)
    fwd = jnp.where(same & (r <= c), 1.0, 0.0).astype(BF16)
    return rev, fwd


def _head_totals(c, col):
    half = lax.broadcasted_iota(jnp.int32, c.shape, 1) < BLOCK
    return jnp.where(half, c[:, col:col + 1], c[:, BLOCK + col:BLOCK + col + 1])


def _chunks(x):
    return [x[:, t * WIDE:(t + 1) * WIDE] for t in range(GROUP)]


def _sb_diag_mask(i, rows=BLOCK):
    t = i * rows + lax.broadcasted_iota(jnp.int32, (rows, SUPER), 0)
    lane = lax.broadcasted_iota(jnp.int32, (rows, SUPER), 1)
    s = ((i * (rows // BLOCK) // GROUP) * GROUP + lane // WIDE) * BLOCK + lane % BLOCK
    return (s < t) & (s >= PAD)


def _sb_logits(q, kc, mask):
    z = _dot_nt(q, kc)
    sp = jnp.log(1.0 + jnp.exp(-jnp.abs(z)))
    lb = jnp.minimum(z, 0.0) - sp
    l1m = lb - z
    if mask is not None:
        l1m = jnp.where(mask, l1m, 0.0)
    hi = l1m.astype(BF16)
    lo = (l1m - hi.astype(F32)).astype(BF16)
    return z, hi, lo, lb


def _sb_suffix(z, hi, lo, U_rev):
    cs = [_dot(h, U_rev) + _dot(l, U_rev) for h, l in zip(_chunks(hi), _chunks(lo))]
    ys, run = [None] * GROUP, None
    for t in reversed(range(GROUP)):
        y = z[:, t * WIDE:(t + 1) * WIDE] + cs[t]
        ys[t] = y if run is None else y + run
        tot = _head_totals(cs[t], 0)
        run = tot if run is None else run + tot
    return jnp.concatenate(ys, axis=1), run


def _super_rows(J):
    return pl.ds(pl.multiple_of(J * SUPER, SUPER), SUPER)


def _sb_fwd(sbq, kcat, vcat):
    L = sbq.shape[0]
    nq = -(-L // QROWS)
    sbq = jnp.pad(sbq, ((0, nq * QROWS - L), (0, 0)))

    def body(q_ref, k_ref, v_ref, o_ref):
        i = pl.program_id(1)
        jd = i * (QROWS // BLOCK) // GROUP
        q = q_ref[...]
        U_rev, _ = _cumsum_matrices()

        def scores(J, mask=None):
            z, hi, lo, _ = _sb_logits(q, k_ref[_super_rows(J), :], mask)
            return _sb_suffix(z, hi, lo, U_rev)

        def weighted(J, y, later, mask=None):
            w = jnp.exp(y + jnp.concatenate([later] * GROUP, axis=1))
            if mask is not None:
                w = jnp.where(mask, w, 0.0)
            return _dot(w.astype(BF16), v_ref[_super_rows(J), :])

        mask = _sb_diag_mask(i, QROWS)
        y, tot = scores(jd, mask)
        acc = weighted(jd, y, jnp.zeros((QROWS, WIDE), F32), mask)
        later = tot
        y, tot = scores(jnp.maximum(jd - 1, 0))

        def step(jj, st):
            acc, later, y, tot = st
            J = jd - 1 - jj
            acc = acc + weighted(J, y, later)
            y_next, tot_next = scores(J - 1)
            return acc, later + tot, y_next, tot_next

        acc, later, y, tot = lax.fori_loop(0, jnp.maximum(jd - 1, 0), step, (acc, later, y, tot))
        last = weighted(0, y, later)
        o_ref[...] = acc + jnp.where(jd >= 1, last, 0.0)

    qspec = pl.BlockSpec((QROWS, BLOCK), lambda p, i: (i, p))
    kvspec = pl.BlockSpec((kcat.shape[0], BLOCK), lambda p, i: (0, p))
    return pl.pallas_call(
        body, name="sb_fwd", grid=(SB_W // BLOCK, nq),
        in_specs=[qspec, kvspec, kvspec], out_specs=qspec,
        out_shape=jax.ShapeDtypeStruct((nq * QROWS, SB_W), F32),
        compiler_params=_cparams("parallel", "arbitrary"),
    )(sbq, kcat, vcat)


def _sb_bwd(sbq, kcat, vcat, do_sb):
    L = sbq.shape[0]
    nq = -(-L // QROWS)
    sbq = jnp.pad(sbq, ((0, nq * QROWS - L), (0, 0)))
    do_sb = jnp.pad(do_sb, ((0, nq * QROWS - L), (0, 0)))
    n_super = _padded_blocks(L) // GROUP
    LP = _padded_blocks(L) * BLOCK

    def body(q_ref, k_hbm, v_hbm, do_ref, dq_ref, dk_hbm, dv_hbm, k_ref, v_ref, dk_ref, dv_ref, g_scr, beta_scr):
        i = pl.program_id(1)
        jd = i * (QROWS // BLOCK) // GROUP
        pair_cols = pl.ds(pl.multiple_of(pl.program_id(0) * BLOCK, BLOCK), BLOCK)

        @pl.when(i == 0)
        def _():
            pltpu.sync_copy(k_hbm.at[:, pair_cols], k_ref)
            pltpu.sync_copy(v_hbm.at[:, pair_cols], v_ref)
            dk_ref[...] = jnp.zeros_like(dk_ref)
            dv_ref[...] = jnp.zeros_like(dv_ref)

        lo = _lane_lo((QROWS, BLOCK))
        q = q_ref[...]
        qf = q.astype(F32)
        do = do_ref[...]
        do_b = do.astype(BF16)
        q_stack = jnp.concatenate([jnp.where(lo, qf, 0.0), jnp.where(lo, 0.0, qf)], axis=0).astype(BF16)
        do_stack = jnp.concatenate([jnp.where(lo, do, 0.0), jnp.where(lo, 0.0, do)], axis=0).astype(BF16)
        U_rev, U_fwd = _cumsum_matrices()
        mask = _sb_diag_mask(i, QROWS)

        def stack(x):
            return jnp.concatenate([x[:, :BLOCK], x[:, BLOCK:]], axis=0)

        def add_key_rows(ref, J, x, other):
            for t, xt in enumerate(_chunks(x)):
                rows = pl.ds(pl.multiple_of((J * GROUP + t) * BLOCK, BLOCK), BLOCK)
                ref[rows, :] += _dot_tn(stack(xt), other)

        def scores(J, mask=None):
            z, hi, lo, lb = _sb_logits(q, k_ref[_super_rows(J), :], mask)
            beta_scr[J] = jnp.exp(lb).astype(BF16)
            return _sb_suffix(z, hi, lo, U_rev)

        def weights(J, y, later, mask=None):
            w = jnp.exp(y + jnp.concatenate([later] * GROUP, axis=1))
            if mask is not None:
                w = jnp.where(mask, w, 0.0)
            g_scr[J] = (_dot_nt(do_b, v_ref[_super_rows(J), :]) * w).astype(BF16)
            add_key_rows(dv_ref, J, w.astype(BF16), do_stack)

        y, tot = scores(jd, mask)
        weights(jd, y, jnp.zeros((QROWS, WIDE), F32), mask)
        later = tot
        y, tot = scores(jnp.maximum(jd - 1, 0))

        def down(jj, st):
            later, y, tot = st
            J = jd - 1 - jj
            weights(J, y, later)
            y_next, tot_next = scores(J - 1)
            return later + tot, y_next, tot_next

        later, y, tot = lax.fori_loop(0, jnp.maximum(jd - 1, 0), down, (later, y, tot))

        @pl.when(jd >= 1)
        def _():
            weights(0, y, later)

        def prefix(J):
            excl, run = [], None
            for g in _chunks(g_scr[J]):
                upto = _dot(g, U_fwd)
                e = upto - g.astype(F32)
                excl.append(e if run is None else e + run)
                tot = _head_totals(upto, BLOCK - 1)
                run = tot if run is None else run + tot
            return jnp.concatenate(excl, axis=1), run

        def grads(J, excl, earlier, mask=None):
            g = g_scr[J].astype(F32)
            beta = beta_scr[J].astype(F32)
            dz = g * (1.0 - beta) - (excl + jnp.concatenate([earlier] * GROUP, axis=1)) * beta
            if mask is not None:
                dz = jnp.where(mask, dz, 0.0)
            dz = dz.astype(BF16)
            add_key_rows(dk_ref, J, dz, q_stack)
            return _dot(dz, k_ref[_super_rows(J), :])

        excl, tot_g = prefix(0)

        def up(J, st):
            dq, earlier, excl, tot_g = st
            dq = dq + grads(J, excl, earlier)
            excl_next, tot_next = prefix(J + 1)
            return dq, earlier + tot_g, excl_next, tot_next

        zero = jnp.zeros((QROWS, WIDE), F32)
        dq, earlier, excl, tot_g = lax.fori_loop(0, jd, up, (jnp.zeros((QROWS, BLOCK), F32), zero, excl, tot_g))
        dq_ref[...] = dq + grads(jd, excl, earlier, mask)

        @pl.when(i == nq - 1)
        def _():
            pltpu.sync_copy(dk_ref, dk_hbm.at[:, pair_cols])
            pltpu.sync_copy(dv_ref, dv_hbm.at[:, pair_cols])

    qspec = pl.BlockSpec((QROWS, BLOCK), lambda p, i: (i, p))
    acc_shape = jax.ShapeDtypeStruct((LP, SB_W), F32)
    return pl.pallas_call(
        body, name="sb_bwd", grid=(SB_W // BLOCK, nq),
        in_specs=[qspec, ANY, ANY, qspec],
        out_specs=[qspec, ANY, ANY],
        out_shape=[jax.ShapeDtypeStruct((nq * QROWS, SB_W), F32), acc_shape, acc_shape],
        scratch_shapes=[pltpu.VMEM((kcat.shape[0], BLOCK), BF16), pltpu.VMEM((kcat.shape[0], BLOCK), BF16),
                        pltpu.VMEM((LP, BLOCK), F32), pltpu.VMEM((LP, BLOCK), F32),
                        pltpu.VMEM((n_super, QROWS, SUPER), BF16), pltpu.VMEM((n_super, QROWS, SUPER), BF16)],
        compiler_params=_cparams("parallel", "arbitrary"),
    )(sbq, kcat, vcat, do_sb)


def _swa_mask(i):
    t = lax.broadcasted_iota(jnp.int32, (BLOCK, 2 * BLOCK), 0)
    s = lax.broadcasted_iota(jnp.int32, (BLOCK, 2 * BLOCK), 1)
    diff = BLOCK + t - s
    return (diff >= 0) & (diff < BLOCK) & ((i - 1) * BLOCK + s >= PAD)


def _swa_probs(q, kk, mask, sink):
    sc = jnp.where(mask, _dot_nt(q, kk), -jnp.inf)
    mx = jnp.maximum(jnp.max(sc, axis=1, keepdims=True), sink)
    e = jnp.exp(sc - mx)
    es = jnp.exp(sink - mx)
    inv = 1.0 / (jnp.sum(e, axis=1, keepdims=True) + es)
    return e * inv, es * inv


def _swa_specs(L):
    prev = lambda w: pl.BlockSpec((BLOCK, w), lambda i: (jnp.maximum(i - 1, 0), 0))
    cur = lambda w: pl.BlockSpec((BLOCK, w), lambda i: (i, 0))
    sink = pl.BlockSpec(memory_space=pltpu.SMEM)
    return [cur(SW_W), prev(SB_W), cur(SB_W), prev(SB_W), cur(SB_W), sink]


def _swa_fwd(swq, swk4, swv4, sinks):
    L = swq.shape[0]

    def body(q_ref, kp_ref, kc_ref, vp_ref, vc_ref, sink_ref, o_ref):
        mask = _swa_mask(pl.program_id(0))
        k2 = jnp.concatenate([kp_ref[...], kc_ref[...]], axis=0)
        v2 = jnp.concatenate([vp_ref[...], vc_ref[...]], axis=0)
        for m in range(SW_W // BLOCK):
            g = m // 4
            sl = slice(m * BLOCK, (m + 1) * BLOCK)
            q = q_ref[:, sl]
            acc = jnp.zeros((BLOCK, BLOCK), F32)
            for par in range(2):
                kv = slice((2 * g + par) * BLOCK, (2 * g + par + 1) * BLOCK)
                p, _ = _swa_probs(q, k2[:, kv], mask, sink_ref[0, 2 * m + par])
                acc = acc + _dot(p.astype(BF16), v2[:, kv])
            o_ref[:, sl] = acc

    return pl.pallas_call(
        body, name="swa_fwd", grid=(L // BLOCK,), in_specs=_swa_specs(L),
        out_specs=pl.BlockSpec((BLOCK, SW_W), lambda i: (i, 0)),
        out_shape=jax.ShapeDtypeStruct((L, SW_W), F32),
        compiler_params=_cparams("parallel"),
    )(swq, swk4, swk4, swv4, swv4, sinks)


def _swa_bwd(swq, swk4, swv4, sinks, do_sw):
    L = swq.shape[0]

    def body(q_ref, kp_ref, kc_ref, vp_ref, vc_ref, sink_ref, do_ref, dq_ref, dkp_ref, dkc_ref, dvp_ref, dvc_ref, ds_ref):
        i = pl.program_id(0)

        @pl.when(i == 0)
        def _():
            ds_ref[...] = jnp.zeros_like(ds_ref)

        mask = _swa_mask(i)
        k2 = jnp.concatenate([kp_ref[...], kc_ref[...]], axis=0)
        v2 = jnp.concatenate([vp_ref[...], vc_ref[...]], axis=0)
        lane = lax.broadcasted_iota(jnp.int32, (8, BLOCK), 1)
        dsink = jnp.zeros((8, BLOCK), F32)
        for g in range(2):
            dkk = [jnp.zeros((2 * BLOCK, BLOCK), F32) for _ in range(2)]
            dvv = [jnp.zeros((2 * BLOCK, BLOCK), F32) for _ in range(2)]
            for m in range(4 * g, 4 * g + 4):
                sl = slice(m * BLOCK, (m + 1) * BLOCK)
                q = q_ref[:, sl]
                do_b = do_ref[:, sl].astype(BF16)
                dq = jnp.zeros((BLOCK, BLOCK), F32)
                for par in range(2):
                    kv = slice((2 * g + par) * BLOCK, (2 * g + par + 1) * BLOCK)
                    kk = k2[:, kv]
                    p, ps = _swa_probs(q, kk, mask, sink_ref[0, 2 * m + par])
                    dp = _dot_nt(do_b, v2[:, kv])
                    dd = jnp.sum(p * dp, axis=1, keepdims=True)
                    ds = (p * (dp - dd)).astype(BF16)
                    dsink = dsink - jnp.where(lane == 2 * m + par, jnp.sum(ps * dd), 0.0)
                    dq = dq + _dot(ds, kk)
                    dkk[par] = dkk[par] + _dot_tn(ds, q)
                    dvv[par] = dvv[par] + _dot_tn(p.astype(BF16), do_b)
                dq_ref[:, sl] = dq
            for par in range(2):
                kv = slice((2 * g + par) * BLOCK, (2 * g + par + 1) * BLOCK)
                dkp_ref[:, kv] = dkk[par][0:BLOCK]
                dkc_ref[:, kv] = dkk[par][BLOCK:2 * BLOCK]
                dvp_ref[:, kv] = dvv[par][0:BLOCK]
                dvc_ref[:, kv] = dvv[par][BLOCK:2 * BLOCK]
        ds_ref[...] += dsink

    part = pl.BlockSpec((BLOCK, SB_W), lambda i: (i, 0))
    row = pl.BlockSpec((BLOCK, SW_W), lambda i: (i, 0))
    pshape = jax.ShapeDtypeStruct((L, SB_W), F32)
    return pl.pallas_call(
        body, name="swa_bwd", grid=(L // BLOCK,), in_specs=_swa_specs(L) + [row],
        out_specs=[row, part, part, part, part, pl.BlockSpec((8, BLOCK), lambda i: (0, 0))],
        out_shape=[jax.ShapeDtypeStruct((L, SW_W), F32), pshape, pshape, pshape, pshape,
                   jax.ShapeDtypeStruct((8, BLOCK), F32)],
        compiler_params=_cparams("arbitrary"),
    )(swq, swk4, swk4, swv4, swv4, sinks, do_sw)


def _gate_specs(tm):
    return [pl.BlockSpec((tm, SW_W), lambda i: (i, 0)), pl.BlockSpec((tm, D_MODEL), lambda i: (i, 1)),
            pl.BlockSpec((tm, D_MODEL), lambda i: (i, 2)), pl.BlockSpec((tm, SB_W), lambda i: (i, 6))]


def _post_fwd(o_sb, o_sw, gate, h0, w_bsb, w_bsw, w_out):
    L = h0.shape[0]
    tm = _row_tile(L, 320)

    def body(osb, osw, swz, gsb, gsw, sbz, h0_ref, wb1, wb2, wo, h1, usb, usw, mrg, ysb, ysw):
        z1 = sbz[...]
        z2 = swz[...]
        u1 = (osb[...] * (z1 * jax.nn.sigmoid(z1))).astype(BF16)
        u2 = (osw[...] * (z2 * jax.nn.sigmoid(z2))).astype(BF16)
        y1 = _dot(u1, wb1[...])
        y2 = _dot(u2, wb2[...])
        merged = (jax.nn.sigmoid(gsb[...]) * y1 + jax.nn.sigmoid(gsw[...]) * y2).astype(BF16)
        h1[...] = h0_ref[...] + _dot(merged, wo[...])
        usb[...] = u1
        usw[...] = u2
        mrg[...] = merged
        ysb[...] = y1
        ysw[...] = y2

    row = lambda w: pl.BlockSpec((tm, w), lambda i: (i, 0))
    whole = lambda a: pl.BlockSpec(a.shape, lambda i: (0, 0))
    shp = lambda w, dt: jax.ShapeDtypeStruct((L, w), dt)
    return pl.pallas_call(
        body, name="post_fwd", grid=(L // tm,),
        in_specs=[row(SB_W), row(SW_W)] + _gate_specs(tm) + [row(D_MODEL), whole(w_bsb), whole(w_bsw), whole(w_out)],
        out_specs=[row(D_MODEL), row(SB_W), row(SW_W), row(D_MODEL), row(D_MODEL), row(D_MODEL)],
        out_shape=[shp(D_MODEL, F32), shp(SB_W, BF16), shp(SW_W, BF16), shp(D_MODEL, BF16),
                   shp(D_MODEL, F32), shp(D_MODEL, F32)],
        compiler_params=_cparams("parallel"),
    )(o_sb, o_sw, gate, gate, gate, gate, h0, w_bsb, w_bsw, w_out)


def _loss_bwd(h1, target, gain2):
    L = h1.shape[0]

    def body(h_ref, t_ref, g_ref, dh_ref, loss_ref, dg_ref):
        i = pl.program_id(0)

        @pl.when(i == 0)
        def _():
            dh_ref[...] = jnp.zeros_like(dh_ref)
            loss_ref[...] = jnp.zeros_like(loss_ref)
            dg_ref[...] = jnp.zeros_like(dg_ref)

        @pl.when(i > 0)
        def _():
            h = h_ref[...]
            g = g_ref[...]
            r = lax.rsqrt(jnp.mean(h * h, axis=-1, keepdims=True) + RMS_EPS)
            n = h * r
            err = n * g - t_ref[...]
            loss_ref[...] += 0.5 * jnp.sum(jnp.mean(err * err, axis=-1, keepdims=True))
            dy = err * (1.0 / D_MODEL)
            dg_ref[...] += jnp.sum(dy * n, axis=0, keepdims=True)
            dn = dy * g
            dh_ref[...] = r * (dn - n * jnp.mean(dn * n, axis=-1, keepdims=True))

    return pl.pallas_call(
        body, name="loss_bwd", grid=(L // BLOCK,),
        in_specs=[pl.BlockSpec((BLOCK, D_MODEL), lambda i: (i, 0)),
                  pl.BlockSpec((BLOCK, D_MODEL), lambda i: (jnp.maximum(i - 1, 0), 0)),
                  pl.BlockSpec((1, D_MODEL), lambda i: (0, 0))],
        out_specs=[pl.BlockSpec((BLOCK, D_MODEL), lambda i: (i, 0)), pl.BlockSpec((1, BLOCK), lambda i: (0, 0)),
                   pl.BlockSpec((1, D_MODEL), lambda i: (0, 0))],
        out_shape=[jax.ShapeDtypeStruct((L, D_MODEL), F32), jax.ShapeDtypeStruct((1, BLOCK), F32),
                   jax.ShapeDtypeStruct((1, D_MODEL), F32)],
        compiler_params=_cparams("arbitrary"),
    )(h1, target, gain2)


def _post_bwd(dh1, gate, y_sb, y_sw, o_sb, o_sw, wt_out, wt_bsb, wt_bsw):
    L = dh1.shape[0]
    tm = _row_tile(L, 320)

    def body(dh, swz, gsb, gsw, sbz, ysb, ysw, osb, osw, wo, wb1, wb2, dy1_ref, dy2_ref, do1_ref, do2_ref, dg_ref):
        dm = _dot(dh[...].astype(BF16), wo[...])
        s1 = jax.nn.sigmoid(gsb[...])
        s2 = jax.nn.sigmoid(gsw[...])
        dy1 = (dm * s1).astype(BF16)
        dy2 = (dm * s2).astype(BF16)
        dy1_ref[...] = dy1
        dy2_ref[...] = dy2
        du1 = _dot(dy1, wb1[...])
        du2 = _dot(dy2, wb2[...])
        z1 = sbz[...]
        z2 = swz[...]
        sz1 = jax.nn.sigmoid(z1)
        sz2 = jax.nn.sigmoid(z2)
        do1_ref[...] = du1 * (z1 * sz1)
        do2_ref[...] = du2 * (z2 * sz2)
        dg_ref[:, 0:SW_W] = (du2 * osw[...] * (sz2 * (1.0 + z2 * (1.0 - sz2)))).astype(BF16)
        dg_ref[:, SW_W:SW_W + D_MODEL] = (dm * ysb[...] * (s1 * (1.0 - s1))).astype(BF16)
        dg_ref[:, SW_W + D_MODEL:SW_W + 2 * D_MODEL] = (dm * ysw[...] * (s2 * (1.0 - s2))).astype(BF16)
        dg_ref[:, SW_W + 2 * D_MODEL:GATE_W] = (du1 * osb[...] * (sz1 * (1.0 + z1 * (1.0 - sz1)))).astype(BF16)

    row = lambda w: pl.BlockSpec((tm, w), lambda i: (i, 0))
    whole = lambda a: pl.BlockSpec(a.shape, lambda i: (0, 0))
    shp = lambda w, dt: jax.ShapeDtypeStruct((L, w), dt)
    return pl.pallas_call(
        body, name="post_bwd", grid=(L // tm,),
        in_specs=[row(D_MODEL)] + _gate_specs(tm) + [row(D_MODEL), row(D_MODEL), row(SB_W), row(SW_W),
                                                     whole(wt_out), whole(wt_bsb), whole(wt_bsw)],
        out_specs=[row(D_MODEL), row(D_MODEL), row(SB_W), row(SW_W), row(GATE_W)],
        out_shape=[shp(D_MODEL, BF16), shp(D_MODEL, BF16), shp(SB_W, F32), shp(SW_W, F32), shp(GATE_W, BF16)],
        compiler_params=_cparams("parallel"),
    )(dh1, gate, gate, gate, gate, y_sb, y_sw, o_sb, o_sw, wt_out, wt_bsb, wt_bsw)


def _assemble_dqkv(dswq, dsbq, dsbk, dsbv, dkp, dkc, dvp, dvc, cos_t, sin_t):
    L = dswq.shape[0]
    tm = BLOCK
    nb = L // BLOCK

    def body(dq_ref, dsq_ref, dsk_ref, dsv_ref, dkp_ref, dkc_ref, dvp_ref, dvc_ref, c_ref, s_ref, o_ref):
        C = c_ref[...]
        S = s_ref[...]
        lane = lax.broadcasted_iota(jnp.int32, (tm, BLOCK), 1)
        lo32 = lane % HEAD_DIM < HEAD_DIM // 2
        lo = lane < HEAD_DIM
        has_next = (pl.program_id(0) + 1 < nb).astype(F32)

        def unrope(dy):
            return dy * C + _rope_partner(dy * S, lo32)

        def fold(cur_ref, next_ref):
            b00, b01, b10, b11 = (cur_ref[:, k * BLOCK:(k + 1) * BLOCK] + has_next * next_ref[:, k * BLOCK:(k + 1) * BLOCK]
                                  for k in range(4))
            return jnp.where(lo, b00 + pltpu.roll(b01, HEAD_DIM, 1), pltpu.roll(b10, HEAD_DIM, 1) + b11)

        for m in range(SW_W // BLOCK):
            sl = slice(m * BLOCK, (m + 1) * BLOCK)
            o_ref[:, sl] = unrope(dq_ref[:, sl] * Q_SCALE).astype(BF16)
        o_ref[:, SW_W:SW_W + SB_W] = (dsq_ref[...] * Q_SCALE).astype(BF16)
        o_ref[:, SW_W + SB_W:SW_W + 2 * SB_W] = dsk_ref[...].astype(BF16)
        o_ref[:, SW_W + 2 * SB_W:SW_W + 3 * SB_W] = dsv_ref[...].astype(BF16)
        base = SW_W + 3 * SB_W
        o_ref[:, base:base + KV_W] = unrope(fold(dkc_ref, dkp_ref)).astype(BF16)
        o_ref[:, base + KV_W:base + 2 * KV_W] = fold(dvc_ref, dvp_ref).astype(BF16)

    row = lambda w: pl.BlockSpec((tm, w), lambda i: (i, 0))
    nxt = pl.BlockSpec((tm, SB_W), lambda i: (jnp.minimum(i + 1, nb - 1), 0))
    return pl.pallas_call(
        body, name="assemble_dqkv", grid=(nb,),
        in_specs=[row(SW_W), row(SB_W), row(SB_W), row(SB_W), nxt, row(SB_W), nxt, row(SB_W), row(BLOCK), row(BLOCK)],
        out_specs=row(QKV_W), out_shape=jax.ShapeDtypeStruct((L, QKV_W), BF16),
        compiler_params=_cparams("parallel"),
    )(dswq, dsbq, dsbk, dsbv, dkp, dkc, dvp, dvc, cos_t, sin_t)


def _norm_bwd(dxn, h0, dh1, gain):
    L = h0.shape[0]

    def body(dx_ref, h_ref, dh_ref, g_ref, gx_ref, dm_ref, dg_ref):
        i = pl.program_id(0)

        @pl.when(i == 0)
        def _():
            dg_ref[...] = jnp.zeros_like(dg_ref)

        h = h_ref[...]
        dxn_t = dx_ref[...]
        r = lax.rsqrt(jnp.mean(h * h, axis=-1, keepdims=True) + RMS_EPS)
        n = h * r
        dg_ref[...] += jnp.sum(dxn_t * n, axis=0, keepdims=True)
        dn = dxn_t * g_ref[...]
        dh0 = dh_ref[...] + r * (dn - n * jnp.mean(dn * n, axis=-1, keepdims=True))
        gx_ref[...] = dh0

        @pl.when(i == 0)
        def _():
            dm_ref[...] = dh0[PAD:BLOCK]

    row = pl.BlockSpec((BLOCK, D_MODEL), lambda i: (i, 0))
    return pl.pallas_call(
        body, name="norm_bwd", grid=(L // BLOCK,),
        in_specs=[row, row, row, pl.BlockSpec((1, D_MODEL), lambda i: (0, 0))],
        out_specs=[pl.BlockSpec((BLOCK, D_MODEL), lambda i: (jnp.maximum(i - 1, 0), 0)),
                   pl.BlockSpec((N_META, D_MODEL), lambda i: (0, 0)), pl.BlockSpec((1, D_MODEL), lambda i: (0, 0))],
        out_shape=[jax.ShapeDtypeStruct((L - BLOCK, D_MODEL), F32), jax.ShapeDtypeStruct((N_META, D_MODEL), F32),
                   jax.ShapeDtypeStruct((1, D_MODEL), F32)],
        compiler_params=_cparams("arbitrary"),
    )(dxn, h0, dh1, gain)


def _adamw(w, g, m, v, name):
    R, C = w.shape
    tr = _row_tile(R, 256) if R % 16 == 0 else R

    def body(w_ref, g_ref, m_ref, v_ref, d_ref, nm_ref, nv_ref):
        g = g_ref[...]
        m_new = ADAM_B1 * m_ref[...] + (1.0 - ADAM_B1) * g
        v_new = ADAM_B2 * v_ref[...] + (1.0 - ADAM_B2) * (g * g)
        m_hat = m_new / (1.0 - ADAM_B1 ** ADAM_STEP)
        v_hat = v_new / (1.0 - ADAM_B2 ** ADAM_STEP)
        d_ref[...] = -ADAM_LR * (m_hat / (jnp.sqrt(v_hat) + ADAM_EPS) + ADAM_WD * w_ref[...])
        nm_ref[...] = m_new
        nv_ref[...] = v_new

    spec = pl.BlockSpec((tr, C), lambda i: (i, 0))
    shp = jax.ShapeDtypeStruct((R, C), F32)
    return pl.pallas_call(
        body, name=name, grid=(R // tr,), in_specs=[spec] * 4, out_specs=[spec] * 3, out_shape=[shp] * 3,
        compiler_params=_cparams("parallel"),
    )(w, g, m, v)


def _sum_leading(a, name):
    n, R, C = a.shape
    tr = _row_tile(R, 160) if R % 16 == 0 else R

    def body(a_ref, o_ref):
        acc = a_ref[0]
        for k in range(1, n):
            acc = acc + a_ref[k]
        o_ref[...] = acc

    return pl.pallas_call(
        body, name=name, grid=(R // tr,), in_specs=[pl.BlockSpec((n, tr, C), lambda i: (0, i, 0))],
        out_specs=pl.BlockSpec((tr, C), lambda i: (i, 0)), out_shape=jax.ShapeDtypeStruct((R, C), a.dtype),
        compiler_params=_cparams("parallel"),
    )(a)


def _place():
    x, y, c = lax.axis_index("x"), lax.axis_index("y"), lax.axis_index("c")
    return x, y, c


def _all_gather(block, name):
    R, C = block.shape

    def body(x_ref, out_ref, send_sems, recv_sems, local_sem):
        x, y, c = _place()
        me, sibling = (x, y, c), (x, y, 1 - c)
        chips = [(1 - x, y), (x, 1 - y), (1 - x, 1 - y)]

        def slot(px, py, pc):
            return out_ref.at[4 * px + 2 * py + pc]

        def copy(k, blk, to, src=None):
            return pltpu.make_async_remote_copy(
                src_ref=slot(*blk) if src is None else src, dst_ref=slot(*blk),
                send_sem=send_sems.at[k], recv_sem=recv_sems.at[k], device_id=to, device_id_type=MESH)

        mine = pltpu.make_async_copy(x_ref, slot(*me), local_sem)
        mine.start()
        first = [copy(0, me, sibling, src=x_ref)]
        first += [copy(1 + j, me, (*chip, c), src=x_ref) for j, chip in enumerate(chips)]
        for cp in first:
            cp.start()
        passed = [copy(4 + j, (*chip, c), sibling) for j, chip in enumerate(chips)]
        for j, chip in enumerate(chips):
            copy(1 + j, (*chip, c), me).wait_recv()
            passed[j].start()
        copy(0, sibling, me).wait_recv()
        for j, chip in enumerate(chips):
            copy(4 + j, (*chip, 1 - c), me).wait_recv()
        for cp in first + passed:
            cp.wait_send()
        mine.wait()

    return pl.pallas_call(
        body, name=name, in_specs=[ANY], out_specs=ANY,
        out_shape=jax.ShapeDtypeStruct((N_DEV, R, C), block.dtype),
        scratch_shapes=[pltpu.SemaphoreType.DMA((7,)), pltpu.SemaphoreType.DMA((7,)), pltpu.SemaphoreType.DMA],
    )(block)


def _pair_exchange(g8):
    _, R, C = g8.shape

    def body(g_ref, out_ref, send_sems, recv_sems):
        x, y, c = _place()
        sibling = (x, y, 1 - c)
        sends = [pltpu.make_async_remote_copy(
            src_ref=g_ref.at[2 * s + (1 - c)], dst_ref=out_ref.at[s], send_sem=send_sems.at[s],
            recv_sem=recv_sems.at[s], device_id=sibling, device_id_type=MESH) for s in range(N_CHIP)]
        for cp in sends:
            cp.start()
        for s in range(N_CHIP):
            pltpu.make_async_remote_copy(
                src_ref=g_ref.at[s], dst_ref=out_ref.at[s], send_sem=send_sems.at[s],
                recv_sem=recv_sems.at[s], device_id=sibling, device_id_type=MESH).wait_recv()
        for cp in sends:
            cp.wait_send()

    return pl.pallas_call(
        body, name="grad_pair_exchange", in_specs=[ANY], out_specs=ANY,
        out_shape=jax.ShapeDtypeStruct((N_CHIP, R, C), g8.dtype),
        scratch_shapes=[pltpu.SemaphoreType.DMA((N_CHIP,)), pltpu.SemaphoreType.DMA((N_CHIP,))],
    )(g8)


def _add_selected(a, b, a_index, name):
    select, sel_fn = a_index
    bs = b if isinstance(b, (list, tuple)) else [b]
    m, R, C = bs[0].shape
    tr = _row_tile(R, 160)

    def body(sel_ref, a_ref, *refs):
        acc = a_ref[...]
        for r in refs[:-1]:
            acc = acc + r[...]
        refs[-1][...] = acc

    blk = lambda fn: pl.BlockSpec((None, tr, C), fn)
    return pl.pallas_call(
        body, name=name,
        grid_spec=pltpu.PrefetchScalarGridSpec(
            num_scalar_prefetch=1, grid=(m, R // tr),
            in_specs=[blk(lambda s, i, sel: (sel_fn(s, sel[0]), i, 0))] + [blk(lambda s, i, sel: (s, i, 0))] * len(bs),
            out_specs=blk(lambda s, i, sel: (s, i, 0))),
        out_shape=jax.ShapeDtypeStruct((m, R, C), a.dtype),
        compiler_params=_cparams("parallel", "parallel"),
    )(jnp.reshape(select, (1,)).astype(jnp.int32), a, *bs)


def _chip_scatter(p4):
    _, R, C = p4.shape

    def body(p_ref, out_x, out_y, out_xy, send_sems, recv_sems):
        x, y, c = _place()
        chips = [(1 - x, y), (x, 1 - y), (1 - x, 1 - y)]
        outs = [out_x, out_y, out_xy]
        sends = [pltpu.make_async_remote_copy(
            src_ref=p_ref.at[2 * cx + cy], dst_ref=outs[j].at[0], send_sem=send_sems.at[j],
            recv_sem=recv_sems.at[j], device_id=(cx, cy, c), device_id_type=MESH) for j, (cx, cy) in enumerate(chips)]
        for cp in sends:
            cp.start()
        for j, (cx, cy) in enumerate(chips):
            pltpu.make_async_remote_copy(
                src_ref=p_ref.at[0], dst_ref=outs[j].at[0], send_sem=send_sems.at[j],
                recv_sem=recv_sems.at[j], device_id=(cx, cy, c), device_id_type=MESH).wait_recv()
        for cp in sends:
            cp.wait_send()

    shp = jax.ShapeDtypeStruct((1, R, C), p4.dtype)
    return pl.pallas_call(
        body, name="grad_chip_scatter", in_specs=[ANY], out_specs=[ANY, ANY, ANY], out_shape=[shp, shp, shp],
        scratch_shapes=[pltpu.SemaphoreType.DMA((3,)), pltpu.SemaphoreType.DMA((3,))],
    )(p4)


def _pair_share(r):
    R, C = r.shape

    def body(r_ref, out_ref, send_sem, recv_sem):
        x, y, c = _place()
        copy = pltpu.make_async_remote_copy(src_ref=r_ref, dst_ref=out_ref, send_sem=send_sem, recv_sem=recv_sem,
                                            device_id=(x, y, 1 - c), device_id_type=MESH)
        copy.start()
        copy.wait()

    return pl.pallas_call(
        body, name="grad_pair_share", in_specs=[ANY], out_specs=ANY,
        out_shape=jax.ShapeDtypeStruct((R, C), r.dtype),
        scratch_shapes=[pltpu.SemaphoreType.DMA, pltpu.SemaphoreType.DMA],
    )(r)


def _gather_direct(block, name):
    R, C = block.shape

    def body(x_ref, out_ref, send_sems, recv_sems, local_sem):
        x, y, c = _place()
        me = 4 * x + 2 * y + c
        local = pltpu.make_async_copy(x_ref, out_ref.at[me], local_sem)
        local.start()
        peers = []
        for k in range(1, N_DEV):
            px, py, pc = x ^ (k >> 2), y ^ ((k >> 1) & 1), c ^ (k & 1)
            peers.append((k, (px, py, pc), 4 * px + 2 * py + pc))
        sends = [pltpu.make_async_remote_copy(
            src_ref=x_ref, dst_ref=out_ref.at[me], send_sem=send_sems.at[k - 1], recv_sem=recv_sems.at[k - 1],
            device_id=dev, device_id_type=MESH) for k, dev, _ in peers]
        for cp in sends:
            cp.start()
        for k, dev, idx in peers:
            pltpu.make_async_remote_copy(
                src_ref=x_ref, dst_ref=out_ref.at[idx], send_sem=send_sems.at[k - 1], recv_sem=recv_sems.at[k - 1],
                device_id=dev, device_id_type=MESH).wait_recv()
        for cp in sends:
            cp.wait_send()
        local.wait()

    return pl.pallas_call(
        body, name=name, in_specs=[ANY], out_specs=ANY,
        out_shape=jax.ShapeDtypeStruct((N_DEV, R, C), block.dtype),
        scratch_shapes=[pltpu.SemaphoreType.DMA((7,)), pltpu.SemaphoreType.DMA((7,)), pltpu.SemaphoreType.DMA],
    )(block)


W_IN_SHARD = IN_COLS // N_CHIP
ROWS_W_IN = (D_MODEL // 2) * W_IN_SHARD // D_MODEL
ROWS_BSB = (SB_W // 2) * (D_MODEL // N_CHIP) // D_MODEL
ROWS_SQ = D_MODEL // N_CHIP // 2
ROWS_BIG = ROWS_W_IN + ROWS_BSB + 2 * ROWS_SQ


def _half(a, c, axis):
    n = a.shape[axis] // 2
    return lax.dynamic_slice_in_dim(a, c * n, n, axis)


def _pack_shard_half(w_in, w_bsb, w_bsw, w_out, c):
    parts = [_half(w_in, c, 0).reshape(ROWS_W_IN, D_MODEL), _half(w_bsb, c, 0).reshape(ROWS_BSB, D_MODEL),
             _half(w_bsw, c, 0), _half(w_out, c, 0)]
    return jnp.concatenate(parts, axis=0)


def _unpack_full(blocks):
    b = blocks.reshape(N_CHIP, 2, ROWS_BIG, D_MODEL)
    o = 0
    w_in = b[:, :, o:o + ROWS_W_IN].reshape(N_CHIP, D_MODEL, W_IN_SHARD)
    o += ROWS_W_IN
    w_bsb = b[:, :, o:o + ROWS_BSB].reshape(N_CHIP, SB_W, D_MODEL // N_CHIP)
    o += ROWS_BSB
    w_bsw = b[:, :, o:o + ROWS_SQ].reshape(D_MODEL, D_MODEL)
    o += ROWS_SQ
    w_out = b[:, :, o:o + ROWS_SQ].reshape(D_MODEL, D_MODEL)
    w_in = jnp.transpose(w_in, (1, 0, 2)).reshape(D_MODEL, IN_COLS)
    w_bsb = jnp.transpose(w_bsb, (1, 0, 2)).reshape(SB_W, D_MODEL)
    return w_in, w_bsb, w_bsw, w_out


def _pack_grads(dw_in, dw_bsb, dw_bsw, dw_out):
    a = jnp.transpose(dw_in.reshape(2, D_MODEL // 2, N_CHIP, W_IN_SHARD), (2, 0, 1, 3)).reshape(N_CHIP, 2, ROWS_W_IN, D_MODEL)
    b = jnp.transpose(dw_bsb.reshape(2, SB_W // 2, N_CHIP, D_MODEL // N_CHIP), (2, 0, 1, 3)).reshape(N_CHIP, 2, ROWS_BSB, D_MODEL)
    c = dw_bsw.reshape(N_CHIP, 2, ROWS_SQ, D_MODEL)
    d = dw_out.reshape(N_CHIP, 2, ROWS_SQ, D_MODEL)
    return jnp.concatenate([a, b, c, d], axis=2).reshape(N_DEV, ROWS_BIG, D_MODEL)


def _unpack_shard(full2):
    o = 0
    w_in = full2[:, o:o + ROWS_W_IN].reshape(D_MODEL, W_IN_SHARD)
    o += ROWS_W_IN
    w_bsb = full2[:, o:o + ROWS_BSB].reshape(SB_W, D_MODEL // N_CHIP)
    o += ROWS_BSB
    w_bsw = full2[:, o:o + ROWS_SQ].reshape(D_MODEL // N_CHIP, D_MODEL)
    o += ROWS_SQ
    w_out = full2[:, o:o + ROWS_SQ].reshape(D_MODEL // N_CHIP, D_MODEL)
    return w_in, w_bsb, w_bsw, w_out


_QKV_FROM_IN = ((1536, 2560), (0, 512), (512, 1024), (1024, 1536), (2560, 2688), (2688, 2816))
_GATE_FROM_IN = ((3328, 4352), (4352, 5376), (5376, 6400), (2816, 3328))


def _split_w_in(w_in):
    qkv = jnp.concatenate([w_in[:, a:b] for a, b in _QKV_FROM_IN], axis=1)
    gate = jnp.concatenate([w_in[:, a:b] for a, b in _GATE_FROM_IN], axis=1)
    return qkv, gate


def _join_dw_in(dqkv, dgate):
    q = lambda a, b: dqkv[:, a:b]
    g = lambda a, b: dgate[:, a:b]
    return jnp.concatenate([q(1024, 1536), q(1536, 2048), q(2048, 2560), q(0, 1024), q(2560, 2688), q(2688, 2816),
                            g(3072, 3584), g(0, 1024), g(1024, 2048), g(2048, 3072)], axis=1)


def _rope_tables(L):
    half = HEAD_DIM // 2
    inv = ROPE_THETA ** (-jnp.arange(half, dtype=F32) / half)
    pos = (jnp.arange(L) - PAD).astype(F32)
    ang = pos[:, None] * inv[None, :]
    cos, sin = jnp.cos(ang), jnp.sin(ang)
    return jnp.tile(jnp.concatenate([cos, cos], axis=1), (1, 2)), jnp.tile(jnp.concatenate([-sin, sin], axis=1), (1, 2))


def kernel(x, meta_tokens, norm_gain, w_in, w_branch_sb, w_branch_swa, w_out, attn_sinks, final_norm_gain, loss_target, m_meta_tokens, m_norm_gain, m_w_in, m_w_branch_sb, m_w_branch_swa, m_w_out, m_attn_sinks, m_final_norm_gain, v_meta_tokens, v_norm_gain, v_w_in, v_w_branch_sb, v_w_branch_swa, v_w_out, v_attn_sinks, v_final_norm_gain):
    xi, yi, ci = _place()
    chip = 2 * xi + yi
    seq = x.shape[1]
    L = seq + BLOCK

    wpack = _pack_shard_half(w_in[0], w_branch_sb[0], w_branch_swa[0], w_out[0], ci).astype(BF16)
    w_in_f, w_bsb_f, w_bsw_f, w_out_f = _unpack_full(_all_gather(wpack, "weight_all_gather"))
    mg = _all_gather(_half(meta_tokens, ci, 0), "meta_all_gather").reshape(N_CHIP, 2, N_META // 2, D_MODEL // N_CHIP)
    meta_full = jnp.transpose(mg, (1, 2, 0, 3)).reshape(N_META, D_MODEL)
    w_qkv, w_gate = _split_w_in(w_in_f)

    h0 = jnp.concatenate([jnp.zeros((PAD, D_MODEL), F32), meta_full, x[0]], axis=0)
    cos_t, sin_t = _rope_tables(L)
    xn = _norm_fwd(h0, norm_gain)
    qkv = _mm([(xn, w_qkv)], F32, "in_proj_qkv", tn_pref=1408)
    gate = _mm([(xn, w_gate)], F32, "in_proj_gate")
    swq, sbq, kcat, vcat, swk4, swv4 = _prep_qkv(qkv, cos_t, sin_t)
    o_sb = _sb_fwd(sbq, kcat, vcat)
    o_sw = _swa_fwd(swq, swk4, swv4, attn_sinks)
    h1, u_sb, u_sw, merged, y_sb, y_sw = _post_fwd(o_sb, o_sw, gate, h0, w_bsb_f, w_bsw_f, w_out_f)

    dh1, loss_part, dgain2 = _loss_bwd(h1, loss_target[0], final_norm_gain.reshape(1, D_MODEL))
    dy_sb, dy_sw, do_sb, do_sw, dgate = _post_bwd(dh1, gate, y_sb, y_sw, o_sb, o_sw, w_out_f.T, w_bsb_f.T, w_bsw_f.T)
    dsbq, dsbk, dsbv = _sb_bwd(sbq, kcat, vcat, do_sb)
    dswq, dkp, dkc, dvp, dvc, dsink = _swa_bwd(swq, swk4, swv4, attn_sinks, do_sw)
    dqkv = _assemble_dqkv(dswq, dsbq, dsbk, dsbv, dkp, dkc, dvp, dvc, cos_t, sin_t)
    dxn = _mm([(dqkv, w_qkv.T), (dgate, w_gate.T)], F32, "dxn", tm_pref=320, tn_pref=1024)
    grad_x, dmeta, dgain1 = _norm_bwd(dxn, h0, dh1, norm_gain)
    dw_qkv = _mm_tn(xn, dqkv, "dw_qkv", tn_pref=1408)
    dw_gate = _mm_tn(xn, dgate, "dw_gate")
    dw_bsb = _mm_tn(u_sb, dy_sb, "dw_branch_sb", tn_pref=1024)
    dw_bsw = _mm_tn(u_sw, dy_sw, "dw_branch_swa", tn_pref=1024)
    dw_out = _mm_tn(merged, dh1, "dw_out", tn_pref=1024)

    g8 = _pack_grads(_join_dw_in(dw_qkv, dw_gate), dw_bsb, dw_bsw, dw_out)
    p4 = _add_selected(g8, _pair_exchange(g8), (ci, lambda s, c: 2 * s + c), "grad_pair_sum")
    r_half = _add_selected(p4, list(_chip_scatter(p4)), (chip, lambda s, own: own), "grad_chip_sum")[0]
    r_sib = _pair_share(r_half)
    full2 = jnp.where(ci == 0, jnp.stack([r_half, r_sib]), jnp.stack([r_sib, r_half]))
    g_in, g_bsb, g_bsw, g_out = _unpack_shard(full2)

    small = jnp.concatenate([dmeta, dgain1, dgain2, jnp.pad(dsink[0:1], ((0, 0), (0, D_MODEL - BLOCK))),
                             jnp.pad(loss_part, ((0, 0), (0, D_MODEL - BLOCK))),
                             jnp.zeros((32 - N_META - 4, D_MODEL), F32)], axis=0)
    tot = _sum_leading(_gather_direct(small, "small_all_gather"), "small_sum")
    g_meta = lax.dynamic_slice_in_dim(tot[0:N_META], chip * (D_MODEL // N_CHIP), D_MODEL // N_CHIP, 1)
    g_gain1 = tot[N_META:N_META + 1]
    g_gain2 = tot[N_META + 1]
    g_sinks = tot[N_META + 2:N_META + 3, 0:16]
    loss = tot[N_META + 3, 0]

    def upd(w, g, m, v, name):
        shape = w.shape
        as2d = lambda a: a.reshape(-1, shape[-1])
        d, nm, nv = _adamw(as2d(w), as2d(g), as2d(m), as2d(v), name)
        return g.reshape(shape), d.reshape(shape), nm.reshape(shape), nv.reshape(shape)

    res = [
        upd(meta_tokens, g_meta, m_meta_tokens, v_meta_tokens, "adamw_meta"),
        upd(norm_gain, g_gain1, m_norm_gain, v_norm_gain, "adamw_norm_gain"),
        upd(w_in, g_in, m_w_in, v_w_in, "adamw_w_in"),
        upd(w_branch_sb, g_bsb, m_w_branch_sb, v_w_branch_sb, "adamw_w_branch_sb"),
        upd(w_branch_swa, g_bsw, m_w_branch_swa, v_w_branch_swa, "adamw_w_branch_swa"),
        upd(w_out, g_out, m_w_out, v_w_out, "adamw_w_out"),
        upd(attn_sinks, g_sinks, m_attn_sinks, v_attn_sinks, "adamw_attn_sinks"),
        upd(final_norm_gain, g_gain2, m_final_norm_gain, v_final_norm_gain, "adamw_final_norm_gain"),
    ]
    grads, deltas, new_m, new_v = zip(*res)
    return (loss, grad_x.reshape(1, seq, D_MODEL), *grads, *deltas, *new_m, *new_v)
```

```python
import functools

import jax
import jax.numpy as jnp
from jax import lax
from jax.experimental import pallas as pl
from jax.experimental.pallas import tpu as pltpu

F32 = jnp.float32
BF16 = jnp.bfloat16

D_MODEL = 1024
BLOCK = 128
N_META = 16
PAD = BLOCK - N_META
HEAD_DIM = 64
SB_W = 512
SW_W = 1024
KV_W = 128
QKV_W = SW_W + 3 * SB_W + 2 * KV_W
GATE_W = 2 * D_MODEL + SW_W + SB_W
IN_COLS = QKV_W + GATE_W
ROPE_THETA = 10000.0
RMS_EPS = 1e-6
Q_SCALE = HEAD_DIM ** -0.5

ADAM_LR = 0.001
ADAM_B1 = 0.9
ADAM_B2 = 0.999
ADAM_EPS = 1e-08
ADAM_WD = 0.01
ADAM_STEP = 10

N_DEV = 8
N_CHIP = 4
MESH = pl.DeviceIdType.MESH
VMEM_LIMIT = 56 * 1024 * 1024
ANY = pl.BlockSpec(memory_space=pl.ANY)


def _cparams(*sem):
    return pltpu.CompilerParams(dimension_semantics=sem, vmem_limit_bytes=VMEM_LIMIT)


def _row_tile(n, pref):
    best = None
    for t in range(16, pref + 1, 16):
        if n % t == 0:
            best = t
    assert best is not None, (n, pref)
    return best


def _col_tile(n, pref):
    best = None
    for t in range(128, pref + 1, 128):
        if n % t == 0:
            best = t
    assert best is not None, (n, pref)
    return best


def _dot(a, b):
    return jnp.dot(a, b, preferred_element_type=F32)


def _dot_nt(a, b):
    return lax.dot_general(a, b, (((1,), (1,)), ((), ())), preferred_element_type=F32)


def _dot_tn(a, b):
    return lax.dot_general(a, b, (((0,), (0,)), ((), ())), preferred_element_type=F32)


def _lane_lo(shape):
    return lax.broadcasted_iota(jnp.int32, shape, len(shape) - 1) % BLOCK < HEAD_DIM


def _mm(pairs, out_dtype, name, tm_pref=640, tn_pref=1792):
    M = pairs[0][0].shape[0]
    N = pairs[0][1].shape[1]
    tm = _row_tile(M, tm_pref)
    tn = _col_tile(N, tn_pref)
    n_pairs = len(pairs)

    def body(*refs):
        o_ref = refs[-1]
        acc = None
        for p in range(n_pairs):
            d = _dot(refs[2 * p][...].astype(BF16), refs[2 * p + 1][...])
            acc = d if acc is None else acc + d
        o_ref[...] = acc.astype(out_dtype)

    in_specs, args = [], []
    for a, b in pairs:
        k = a.shape[1]
        in_specs += [pl.BlockSpec((tm, k), lambda n, m: (m, 0)), pl.BlockSpec((k, tn), lambda n, m: (0, n))]
        args += [a, b]
    return pl.pallas_call(
        body, name=name, grid=(N // tn, M // tm), in_specs=in_specs,
        out_specs=pl.BlockSpec((tm, tn), lambda n, m: (m, n)),
        out_shape=jax.ShapeDtypeStruct((M, N), out_dtype),
        compiler_params=_cparams("parallel", "arbitrary"),
    )(*args)


def _mm_tn(a, b, name, tn_pref=1792, tl_pref=640):
    L, M = a.shape
    N = b.shape[1]
    tn = _col_tile(N, tn_pref)
    tl = _row_tile(L, tl_pref)

    def body(a_ref, b_ref, o_ref):
        @pl.when(pl.program_id(1) == 0)
        def _():
            o_ref[...] = jnp.zeros_like(o_ref)
        o_ref[...] += _dot_tn(a_ref[...].astype(BF16), b_ref[...].astype(BF16))

    return pl.pallas_call(
        body, name=name, grid=(N // tn, L // tl),
        in_specs=[pl.BlockSpec((tl, M), lambda n, l: (l, 0)), pl.BlockSpec((tl, tn), lambda n, l: (l, n))],
        out_specs=pl.BlockSpec((M, tn), lambda n, l: (0, n)),
        out_shape=jax.ShapeDtypeStruct((M, N), F32),
        compiler_params=_cparams("parallel", "arbitrary"),
    )(a, b)


def _norm_fwd(h0, gain):
    L = h0.shape[0]
    tm = _row_tile(L, 640)

    def body(h_ref, g_ref, o_ref):
        h = h_ref[...]
        r = lax.rsqrt(jnp.mean(h * h, axis=-1, keepdims=True) + RMS_EPS)
        o_ref[...] = ((h * r) * g_ref[...]).astype(BF16)

    return pl.pallas_call(
        body, name="norm_fwd", grid=(L // tm,),
        in_specs=[pl.BlockSpec((tm, D_MODEL), lambda i: (i, 0)), pl.BlockSpec((1, D_MODEL), lambda i: (0, 0))],
        out_specs=pl.BlockSpec((tm, D_MODEL), lambda i: (i, 0)),
        out_shape=jax.ShapeDtypeStruct((L, D_MODEL), BF16),
        compiler_params=_cparams("parallel"),
    )(h0, gain)


def _rope_partner(x, lo32):
    return jnp.where(lo32, pltpu.roll(x, 96, 1), pltpu.roll(x, 32, 1))


def _prep_qkv(qkv, cos_t, sin_t):
    L = qkv.shape[0]
    tm = BLOCK * max(d for d in range(1, 6) if (L // BLOCK) % d == 0)
    n_steps = L // tm
    assert L // BLOCK + tm // BLOCK >= _padded_blocks(L)

    def body(x_ref, c_ref, s_ref, swq, sbq, kcat, vcat, swk4, swv4):
        C = c_ref[...]
        S = s_ref[...]
        lane = lax.broadcasted_iota(jnp.int32, (tm, BLOCK), 1)
        lo32 = lane % HEAD_DIM < HEAD_DIM // 2
        lo = lane < HEAD_DIM

        def rope(x):
            return x * C + _rope_partner(x, lo32) * S

        for m in range(SW_W // BLOCK):
            sl = slice(m * BLOCK, (m + 1) * BLOCK)
            swq[:, sl] = (rope(x_ref[:, sl]) * Q_SCALE).astype(BF16)
        for m in range(SB_W // BLOCK):
            sl = slice(m * BLOCK, (m + 1) * BLOCK)
            sbq[:, sl] = (x_ref[:, SW_W + m * BLOCK:SW_W + (m + 1) * BLOCK] * Q_SCALE).astype(BF16)
            k = x_ref[:, SW_W + SB_W + m * BLOCK:SW_W + SB_W + (m + 1) * BLOCK]
            v = x_ref[:, SW_W + 2 * SB_W + m * BLOCK:SW_W + 2 * SB_W + (m + 1) * BLOCK]
            for src, dst in ((k, kcat), (v, vcat)):
                even = jnp.where(lo, src, 0.0).astype(BF16)
                odd = jnp.where(lo, 0.0, src).astype(BF16)
                for b in range(tm // BLOCK):
                    dst[2 * b * BLOCK:(2 * b + 1) * BLOCK, sl] = even[b * BLOCK:(b + 1) * BLOCK]
                    dst[(2 * b + 1) * BLOCK:(2 * b + 2) * BLOCK, sl] = odd[b * BLOCK:(b + 1) * BLOCK]
        base = SW_W + 3 * SB_W
        kx = rope(x_ref[:, base:base + KV_W])
        vx = x_ref[:, base + KV_W:base + 2 * KV_W]
        for src, dst in ((kx, swk4), (vx, swv4)):
            sw = pltpu.roll(src, HEAD_DIM, 1)
            dst[:, 0:128] = jnp.where(lo, src, 0.0).astype(BF16)
            dst[:, 128:256] = jnp.where(lo, 0.0, sw).astype(BF16)
            dst[:, 256:384] = jnp.where(lo, sw, 0.0).astype(BF16)
            dst[:, 384:512] = jnp.where(lo, 0.0, src).astype(BF16)

        @pl.when(pl.program_id(0) == n_steps)
        def _():
            kcat[...] = jnp.zeros_like(kcat)
            vcat[...] = jnp.zeros_like(vcat)

    row = lambda w: pl.BlockSpec((tm, w), lambda i: (jnp.minimum(i, n_steps - 1), 0))
    row2 = pl.BlockSpec((2 * tm, SB_W), lambda i: (i, 0))
    shp = lambda r, w: jax.ShapeDtypeStruct((r, w), BF16)
    return pl.pallas_call(
        body, name="prep_qkv", grid=(n_steps + 1,),
        in_specs=[row(QKV_W), row(BLOCK), row(BLOCK)],
        out_specs=[row(SW_W), row(SB_W), row2, row2, row(SB_W), row(SB_W)],
        out_shape=[shp(L, SW_W), shp(L, SB_W), shp(2 * (L + tm), SB_W), shp(2 * (L + tm), SB_W), shp(L, SB_W),
                   shp(L, SB_W)],
        compiler_params=_cparams("arbitrary"),
    )(qkv, cos_t, sin_t)


WIDE = 2 * BLOCK
GROUP = 4
SUPER = GROUP * WIDE
QROWS = 2 * BLOCK
QROWS_FWD = 4 * BLOCK


def _padded_blocks(L):
    return -(-(L // BLOCK) // GROUP) * GROUP


def _cumsum_matrices():
    r = lax.broadcasted_iota(jnp.int32, (WIDE, WIDE), 0)
    c = lax.broadcasted_iota(jnp.int32, (WIDE, WIDE), 1)
    same = (r < BLOCK) == (c < BLOCK)
    rev = jnp.where(same & (r >= c), 1.0, 0.0).astype(BF16)
    fwd = jnp.where(same & (r <= c), 1.0, 0.0).astype(BF16)
    return rev, fwd


def _head_totals(c, col):
    half = lax.broadcasted_iota(jnp.int32, c.shape, 1) < BLOCK
    return jnp.where(half, c[:, col:col + 1], c[:, BLOCK + col:BLOCK + col + 1])


def _chunks(x):
    return [x[:, t * WIDE:(t + 1) * WIDE] for t in range(GROUP)]


def _sb_diag_mask(i, rows=BLOCK):
    t = i * rows + lax.broadcasted_iota(jnp.int32, (rows, SUPER), 0)
    lane = lax.broadcasted_iota(jnp.int32, (rows, SUPER), 1)
    s = ((i * (rows // BLOCK) // GROUP) * GROUP + lane // WIDE) * BLOCK + lane % BLOCK
    return (s < t) & (s >= PAD)


def _sb_logits(q, kc, mask):
    z = _dot_nt(q, kc)
    sp = jnp.log(1.0 + jnp.exp(-jnp.abs(z)))
    lb = jnp.minimum(z, 0.0) - sp
    l1m = lb - z
    if mask is not None:
        l1m = jnp.where(mask, l1m, 0.0)
    hi = l1m.astype(BF16)
    lo = (l1m - hi.astype(F32)).astype(BF16)
    return z, hi, lo, lb


def _sb_suffix(z, hi, lo, U_rev):
    cs = [_dot(h, U_rev) + _dot(l, U_rev) for h, l in zip(_chunks(hi), _chunks(lo))]
    ys, run = [None] * GROUP, None
    for t in reversed(range(GROUP)):
        y = z[:, t * WIDE:(t + 1) * WIDE] + cs[t]
        ys[t] = y if run is None else y + run
        tot = _head_totals(cs[t], 0)
        run = tot if run is None else run + tot
    return jnp.concatenate(ys, axis=1), run


def _super_rows(J):
    return pl.ds(pl.multiple_of(J * SUPER, SUPER), SUPER)


def _sb_fwd(sbq, kcat, vcat):
    L = sbq.shape[0]
    nq = -(-L // QROWS_FWD)
    sbq = jnp.pad(sbq, ((0, nq * QROWS_FWD - L), (0, 0)))

    def body(q_ref, k_ref, v_ref, o_ref):
        i = pl.program_id(1)
        jd = i * (QROWS_FWD // BLOCK) // GROUP
        q = q_ref[...]
        U_rev, _ = _cumsum_matrices()

        def scores(J, mask=None):
            z, hi, lo, _ = _sb_logits(q, k_ref[_super_rows(J), :], mask)
            return _sb_suffix(z, hi, lo, U_rev)

        def weighted(J, y, later, mask=None):
            w = jnp.exp(y + jnp.concatenate([later] * GROUP, axis=1))
            if mask is not None:
                w = jnp.where(mask, w, 0.0)
            return _dot(w.astype(BF16), v_ref[_super_rows(J), :])

        mask = _sb_diag_mask(i, QROWS_FWD)
        y, tot = scores(jd, mask)
        acc = weighted(jd, y, jnp.zeros((QROWS_FWD, WIDE), F32), mask)
        later = tot
        y, tot = scores(jnp.maximum(jd - 1, 0))

        def step(jj, st):
            acc, later, y, tot = st
            J = jd - 1 - jj
            acc = acc + weighted(J, y, later)
            y_next, tot_next = scores(J - 1)
            return acc, later + tot, y_next, tot_next

        acc, later, y, tot = lax.fori_loop(0, jnp.maximum(jd - 1, 0), step, (acc, later, y, tot))
        last = weighted(0, y, later)
        o_ref[...] = acc + jnp.where(jd >= 1, last, 0.0)

    qspec = pl.BlockSpec((QROWS_FWD, BLOCK), lambda p, i: (i, p))
    kvspec = pl.BlockSpec((kcat.shape[0], BLOCK), lambda p, i: (0, p))
    return pl.pallas_call(
        body, name="sb_fwd", grid=(SB_W // BLOCK, nq),
        in_specs=[qspec, kvspec, kvspec], out_specs=qspec,
        out_shape=jax.ShapeDtypeStruct((nq * QROWS_FWD, SB_W), F32),
        compiler_params=_cparams("parallel", "arbitrary"),
    )(sbq, kcat, vcat)


def _sb_bwd(sbq, kcat, vcat, do_sb):
    L = sbq.shape[0]
    nq = -(-L // QROWS)
    sbq = jnp.pad(sbq, ((0, nq * QROWS - L), (0, 0)))
    do_sb = jnp.pad(do_sb, ((0, nq * QROWS - L), (0, 0)))
    n_super = _padded_blocks(L) // GROUP
    LP = _padded_blocks(L) * BLOCK

    def body(q_ref, k_hbm, v_hbm, do_ref, dq_ref, dk_hbm, dv_hbm, k_ref, v_ref, dk_ref, dv_ref, g_scr, beta_scr):
        i = pl.program_id(1)
        jd = i * (QROWS // BLOCK) // GROUP
        pair_cols = pl.ds(pl.multiple_of(pl.program_id(0) * BLOCK, BLOCK), BLOCK)

        @pl.when(i == 0)
        def _():
            pltpu.sync_copy(k_hbm.at[:, pair_cols], k_ref)
            pltpu.sync_copy(v_hbm.at[:, pair_cols], v_ref)
            dk_ref[...] = jnp.zeros_like(dk_ref)
            dv_ref[...] = jnp.zeros_like(dv_ref)

        lo = _lane_lo((QROWS, BLOCK))
        q = q_ref[...]
        qf = q.astype(F32)
        do = do_ref[...]
        do_b = do.astype(BF16)
        q_stack = jnp.concatenate([jnp.where(lo, qf, 0.0), jnp.where(lo, 0.0, qf)], axis=0).astype(BF16)
        do_stack = jnp.concatenate([jnp.where(lo, do, 0.0), jnp.where(lo, 0.0, do)], axis=0).astype(BF16)
        U_rev, U_fwd = _cumsum_matrices()
        mask = _sb_diag_mask(i, QROWS)

        def stack(x):
            return jnp.concatenate([x[:, :BLOCK], x[:, BLOCK:]], axis=0)

        def add_key_rows(ref, J, x, other):
            for t, xt in enumerate(_chunks(x)):
                rows = pl.ds(pl.multiple_of((J * GROUP + t) * BLOCK, BLOCK), BLOCK)
                ref[rows, :] += _dot_tn(stack(xt), other)

        def scores(J, mask=None):
            z, hi, lo, lb = _sb_logits(q, k_ref[_super_rows(J), :], mask)
            beta_scr[J] = jnp.exp(lb).astype(BF16)
            return _sb_suffix(z, hi, lo, U_rev)

        def weights(J, y, later, mask=None):
            w = jnp.exp(y + jnp.concatenate([later] * GROUP, axis=1))
            if mask is not None:
                w = jnp.where(mask, w, 0.0)
            g_scr[J] = (_dot_nt(do_b, v_ref[_super_rows(J), :]) * w).astype(BF16)
            add_key_rows(dv_ref, J, w.astype(BF16), do_stack)

        y, tot = scores(jd, mask)
        weights(jd, y, jnp.zeros((QROWS, WIDE), F32), mask)
        later = tot
        y, tot = scores(jnp.maximum(jd - 1, 0))

        def down(jj, st):
            later, y, tot = st
            J = jd - 1 - jj
            weights(J, y, later)
            y_next, tot_next = scores(J - 1)
            return later + tot, y_next, tot_next

        later, y, tot = lax.fori_loop(0, jnp.maximum(jd - 1, 0), down, (later, y, tot))

        @pl.when(jd >= 1)
        def _():
            weights(0, y, later)

        def prefix(J):
            excl, run = [], None
            for g in _chunks(g_scr[J]):
                upto = _dot(g, U_fwd)
                e = upto - g.astype(F32)
                excl.append(e if run is None else e + run)
                tot = _head_totals(upto, BLOCK - 1)
                run = tot if run is None else run + tot
            return jnp.concatenate(excl, axis=1), run

        def grads(J, excl, earlier, mask=None):
            g = g_scr[J].astype(F32)
            beta = beta_scr[J].astype(F32)
            dz = g * (1.0 - beta) - (excl + jnp.concatenate([earlier] * GROUP, axis=1)) * beta
            if mask is not None:
                dz = jnp.where(mask, dz, 0.0)
            dz = dz.astype(BF16)
            add_key_rows(dk_ref, J, dz, q_stack)
            return _dot(dz, k_ref[_super_rows(J), :])

        excl, tot_g = prefix(0)

        def up(J, st):
            dq, earlier, excl, tot_g = st
            dq = dq + grads(J, excl, earlier)
            excl_next, tot_next = prefix(J + 1)
            return dq, earlier + tot_g, excl_next, tot_next

        zero = jnp.zeros((QROWS, WIDE), F32)
        dq, earlier, excl, tot_g = lax.fori_loop(0, jd, up, (jnp.zeros((QROWS, BLOCK), F32), zero, excl, tot_g))
        dq_ref[...] = dq + grads(jd, excl, earlier, mask)

        @pl.when(i == nq - 1)
        def _():
            pltpu.sync_copy(dk_ref, dk_hbm.at[:, pair_cols])
            pltpu.sync_copy(dv_ref, dv_hbm.at[:, pair_cols])

    qspec = pl.BlockSpec((QROWS, BLOCK), lambda p, i: (i, p))
    acc_shape = jax.ShapeDtypeStruct((LP, SB_W), F32)
    return pl.pallas_call(
        body, name="sb_bwd", grid=(SB_W // BLOCK, nq),
        in_specs=[qspec, ANY, ANY, qspec],
        out_specs=[qspec, ANY, ANY],
        out_shape=[jax.ShapeDtypeStruct((nq * QROWS, SB_W), F32), acc_shape, acc_shape],
        scratch_shapes=[pltpu.VMEM((kcat.shape[0], BLOCK), BF16), pltpu.VMEM((kcat.shape[0], BLOCK), BF16),
                        pltpu.VMEM((LP, BLOCK), F32), pltpu.VMEM((LP, BLOCK), F32),
                        pltpu.VMEM((n_super, QROWS, SUPER), BF16), pltpu.VMEM((n_super, QROWS, SUPER), BF16)],
        compiler_params=_cparams("parallel", "arbitrary"),
    )(sbq, kcat, vcat, do_sb)


def _swa_mask(i):
    t = lax.broadcasted_iota(jnp.int32, (BLOCK, 2 * BLOCK), 0)
    s = lax.broadcasted_iota(jnp.int32, (BLOCK, 2 * BLOCK), 1)
    diff = BLOCK + t - s
    return (diff >= 0) & (diff < BLOCK) & ((i - 1) * BLOCK + s >= PAD)


def _swa_probs(q, kk, mask, sink):
    sc = jnp.where(mask, _dot_nt(q, kk), -jnp.inf)
    mx = jnp.maximum(jnp.max(sc, axis=1, keepdims=True), sink)
    e = jnp.exp(sc - mx)
    es = jnp.exp(sink - mx)
    inv = 1.0 / (jnp.sum(e, axis=1, keepdims=True) + es)
    return e * inv, es * inv


def _swa_specs(L):
    prev = lambda w: pl.BlockSpec((BLOCK, w), lambda i: (jnp.maximum(i - 1, 0), 0))
    cur = lambda w: pl.BlockSpec((BLOCK, w), lambda i: (i, 0))
    sink = pl.BlockSpec(memory_space=pltpu.SMEM)
    return [cur(SW_W), prev(SB_W), cur(SB_W), prev(SB_W), cur(SB_W), sink]


def _swa_fwd(swq, swk4, swv4, sinks):
    L = swq.shape[0]

    def body(q_ref, kp_ref, kc_ref, vp_ref, vc_ref, sink_ref, o_ref):
        mask = _swa_mask(pl.program_id(0))
        k2 = jnp.concatenate([kp_ref[...], kc_ref[...]], axis=0)
        v2 = jnp.concatenate([vp_ref[...], vc_ref[...]], axis=0)
        for m in range(SW_W // BLOCK):
            g = m // 4
            sl = slice(m * BLOCK, (m + 1) * BLOCK)
            q = q_ref[:, sl]
            acc = jnp.zeros((BLOCK, BLOCK), F32)
            for par in range(2):
                kv = slice((2 * g + par) * BLOCK, (2 * g + par + 1) * BLOCK)
                p, _ = _swa_probs(q, k2[:, kv], mask, sink_ref[0, 2 * m + par])
                acc = acc + _dot(p.astype(BF16), v2[:, kv])
            o_ref[:, sl] = acc

    return pl.pallas_call(
        body, name="swa_fwd", grid=(L // BLOCK,), in_specs=_swa_specs(L),
        out_specs=pl.BlockSpec((BLOCK, SW_W), lambda i: (i, 0)),
        out_shape=jax.ShapeDtypeStruct((L, SW_W), F32),
        compiler_params=_cparams("parallel"),
    )(swq, swk4, swk4, swv4, swv4, sinks)


def _swa_bwd(swq, swk4, swv4, sinks, do_sw):
    L = swq.shape[0]

    def body(q_ref, kp_ref, kc_ref, vp_ref, vc_ref, sink_ref, do_ref, dq_ref, dkp_ref, dkc_ref, dvp_ref, dvc_ref, ds_ref):
        i = pl.program_id(0)

        @pl.when(i == 0)
        def _():
            ds_ref[...] = jnp.zeros_like(ds_ref)

        mask = _swa_mask(i)
        k2 = jnp.concatenate([kp_ref[...], kc_ref[...]], axis=0)
        v2 = jnp.concatenate([vp_ref[...], vc_ref[...]], axis=0)
        lane = lax.broadcasted_iota(jnp.int32, (8, BLOCK), 1)
        dsink = jnp.zeros((8, BLOCK), F32)
        for g in range(2):
            dkk = [jnp.zeros((2 * BLOCK, BLOCK), F32) for _ in range(2)]
            dvv = [jnp.zeros((2 * BLOCK, BLOCK), F32) for _ in range(2)]
            for m in range(4 * g, 4 * g + 4):
                sl = slice(m * BLOCK, (m + 1) * BLOCK)
                q = q_ref[:, sl]
                do_b = do_ref[:, sl].astype(BF16)
                dq = jnp.zeros((BLOCK, BLOCK), F32)
                for par in range(2):
                    kv = slice((2 * g + par) * BLOCK, (2 * g + par + 1) * BLOCK)
                    kk = k2[:, kv]
                    p, ps = _swa_probs(q, kk, mask, sink_ref[0, 2 * m + par])
                    dp = _dot_nt(do_b, v2[:, kv])
                    dd = jnp.sum(p * dp, axis=1, keepdims=True)
                    ds = (p * (dp - dd)).astype(BF16)
                    dsink = dsink - jnp.where(lane == 2 * m + par, jnp.sum(ps * dd), 0.0)
                    dq = dq + _dot(ds, kk)
                    dkk[par] = dkk[par] + _dot_tn(ds, q)
                    dvv[par] = dvv[par] + _dot_tn(p.astype(BF16), do_b)
                dq_ref[:, sl] = dq
            for par in range(2):
                kv = slice((2 * g + par) * BLOCK, (2 * g + par + 1) * BLOCK)
                dkp_ref[:, kv] = dkk[par][0:BLOCK]
                dkc_ref[:, kv] = dkk[par][BLOCK:2 * BLOCK]
                dvp_ref[:, kv] = dvv[par][0:BLOCK]
                dvc_ref[:, kv] = dvv[par][BLOCK:2 * BLOCK]
        ds_ref[...] += dsink

    part = pl.BlockSpec((BLOCK, SB_W), lambda i: (i, 0))
    row = pl.BlockSpec((BLOCK, SW_W), lambda i: (i, 0))
    pshape = jax.ShapeDtypeStruct((L, SB_W), F32)
    return pl.pallas_call(
        body, name="swa_bwd", grid=(L // BLOCK,), in_specs=_swa_specs(L) + [row],
        out_specs=[row, part, part, part, part, pl.BlockSpec((8, BLOCK), lambda i: (0, 0))],
        out_shape=[jax.ShapeDtypeStruct((L, SW_W), F32), pshape, pshape, pshape, pshape,
                   jax.ShapeDtypeStruct((8, BLOCK), F32)],
        compiler_params=_cparams("arbitrary"),
    )(swq, swk4, swk4, swv4, swv4, sinks, do_sw)


def _gate_specs(tm):
    return [pl.BlockSpec((tm, SW_W), lambda i: (i, 0)), pl.BlockSpec((tm, D_MODEL), lambda i: (i, 1)),
            pl.BlockSpec((tm, D_MODEL), lambda i: (i, 2)), pl.BlockSpec((tm, SB_W), lambda i: (i, 6))]


def _post_fwd(o_sb, o_sw, gate, h0, w_bsb, w_bsw, w_out):
    L = h0.shape[0]
    tm = _row_tile(L, 320)

    def body(osb, osw, swz, gsb, gsw, sbz, h0_ref, wb1, wb2, wo, h1, usb, usw, mrg, ysb, ysw):
        z1 = sbz[...]
        z2 = swz[...]
        u1 = (osb[...] * (z1 * jax.nn.sigmoid(z1))).astype(BF16)
        u2 = (osw[...] * (z2 * jax.nn.sigmoid(z2))).astype(BF16)
        y1 = _dot(u1, wb1[...])
        y2 = _dot(u2, wb2[...])
        merged = (jax.nn.sigmoid(gsb[...]) * y1 + jax.nn.sigmoid(gsw[...]) * y2).astype(BF16)
        h1[...] = h0_ref[...] + _dot(merged, wo[...])
        usb[...] = u1
        usw[...] = u2
        mrg[...] = merged
        ysb[...] = y1
        ysw[...] = y2

    row = lambda w: pl.BlockSpec((tm, w), lambda i: (i, 0))
    whole = lambda a: pl.BlockSpec(a.shape, lambda i: (0, 0))
    shp = lambda w, dt: jax.ShapeDtypeStruct((L, w), dt)
    return pl.pallas_call(
        body, name="post_fwd", grid=(L // tm,),
        in_specs=[row(SB_W), row(SW_W)] + _gate_specs(tm) + [row(D_MODEL), whole(w_bsb), whole(w_bsw), whole(w_out)],
        out_specs=[row(D_MODEL), row(SB_W), row(SW_W), row(D_MODEL), row(D_MODEL), row(D_MODEL)],
        out_shape=[shp(D_MODEL, F32), shp(SB_W, BF16), shp(SW_W, BF16), shp(D_MODEL, BF16),
                   shp(D_MODEL, F32), shp(D_MODEL, F32)],
        compiler_params=_cparams("parallel"),
    )(o_sb, o_sw, gate, gate, gate, gate, h0, w_bsb, w_bsw, w_out)


def _loss_bwd(h1, target, gain2):
    L = h1.shape[0]

    def body(h_ref, t_ref, g_ref, dh_ref, loss_ref, dg_ref):
        i = pl.program_id(0)

        @pl.when(i == 0)
        def _():
            dh_ref[...] = jnp.zeros_like(dh_ref)
            loss_ref[...] = jnp.zeros_like(loss_ref)
            dg_ref[...] = jnp.zeros_like(dg_ref)

        @pl.when(i > 0)
        def _():
            h = h_ref[...]
            g = g_ref[...]
            r = lax.rsqrt(jnp.mean(h * h, axis=-1, keepdims=True) + RMS_EPS)
            n = h * r
            err = n * g - t_ref[...]
            loss_ref[...] += 0.5 * jnp.sum(jnp.mean(err * err, axis=-1, keepdims=True))
            dy = err * (1.0 / D_MODEL)
            dg_ref[...] += jnp.sum(dy * n, axis=0, keepdims=True)
            dn = dy * g
            dh_ref[...] = r * (dn - n * jnp.mean(dn * n, axis=-1, keepdims=True))

    return pl.pallas_call(
        body, name="loss_bwd", grid=(L // BLOCK,),
        in_specs=[pl.BlockSpec((BLOCK, D_MODEL), lambda i: (i, 0)),
                  pl.BlockSpec((BLOCK, D_MODEL), lambda i: (jnp.maximum(i - 1, 0), 0)),
                  pl.BlockSpec((1, D_MODEL), lambda i: (0, 0))],
        out_specs=[pl.BlockSpec((BLOCK, D_MODEL), lambda i: (i, 0)), pl.BlockSpec((1, BLOCK), lambda i: (0, 0)),
                   pl.BlockSpec((1, D_MODEL), lambda i: (0, 0))],
        out_shape=[jax.ShapeDtypeStruct((L, D_MODEL), F32), jax.ShapeDtypeStruct((1, BLOCK), F32),
                   jax.ShapeDtypeStruct((1, D_MODEL), F32)],
        compiler_params=_cparams("arbitrary"),
    )(h1, target, gain2)


def _post_bwd(dh1, gate, y_sb, y_sw, o_sb, o_sw, wt_out, wt_bsb, wt_bsw):
    L = dh1.shape[0]
    tm = _row_tile(L, 320)

    def body(dh, swz, gsb, gsw, sbz, ysb, ysw, osb, osw, wo, wb1, wb2, dy1_ref, dy2_ref, do1_ref, do2_ref, dg_ref):
        dm = _dot(dh[...].astype(BF16), wo[...])
        s1 = jax.nn.sigmoid(gsb[...])
        s2 = jax.nn.sigmoid(gsw[...])
        dy1 = (dm * s1).astype(BF16)
        dy2 = (dm * s2).astype(BF16)
        dy1_ref[...] = dy1
        dy2_ref[...] = dy2
        du1 = _dot(dy1, wb1[...])
        du2 = _dot(dy2, wb2[...])
        z1 = sbz[...]
        z2 = swz[...]
        sz1 = jax.nn.sigmoid(z1)
        sz2 = jax.nn.sigmoid(z2)
        do1_ref[...] = du1 * (z1 * sz1)
        do2_ref[...] = du2 * (z2 * sz2)
        dg_ref[:, 0:SW_W] = (du2 * osw[...] * (sz2 * (1.0 + z2 * (1.0 - sz2)))).astype(BF16)
        dg_ref[:, SW_W:SW_W + D_MODEL] = (dm * ysb[...] * (s1 * (1.0 - s1))).astype(BF16)
        dg_ref[:, SW_W + D_MODEL:SW_W + 2 * D_MODEL] = (dm * ysw[...] * (s2 * (1.0 - s2))).astype(BF16)
        dg_ref[:, SW_W + 2 * D_MODEL:GATE_W] = (du1 * osb[...] * (sz1 * (1.0 + z1 * (1.0 - sz1)))).astype(BF16)

    row = lambda w: pl.BlockSpec((tm, w), lambda i: (i, 0))
    whole = lambda a: pl.BlockSpec(a.shape, lambda i: (0, 0))
    shp = lambda w, dt: jax.ShapeDtypeStruct((L, w), dt)
    return pl.pallas_call(
        body, name="post_bwd", grid=(L // tm,),
        in_specs=[row(D_MODEL)] + _gate_specs(tm) + [row(D_MODEL), row(D_MODEL), row(SB_W), row(SW_W),
                                                     whole(wt_out), whole(wt_bsb), whole(wt_bsw)],
        out_specs=[row(D_MODEL), row(D_MODEL), row(SB_W), row(SW_W), row(GATE_W)],
        out_shape=[shp(D_MODEL, BF16), shp(D_MODEL, BF16), shp(SB_W, F32), shp(SW_W, F32), shp(GATE_W, BF16)],
        compiler_params=_cparams("parallel"),
    )(dh1, gate, gate, gate, gate, y_sb, y_sw, o_sb, o_sw, wt_out, wt_bsb, wt_bsw)


def _assemble_dqkv(dswq, dsbq, dsbk, dsbv, dkp, dkc, dvp, dvc, cos_t, sin_t):
    L = dswq.shape[0]
    tm = BLOCK
    nb = L // BLOCK

    def body(dq_ref, dsq_ref, dsk_ref, dsv_ref, dkp_ref, dkc_ref, dvp_ref, dvc_ref, c_ref, s_ref, o_ref):
        C = c_ref[...]
        S = s_ref[...]
        lane = lax.broadcasted_iota(jnp.int32, (tm, BLOCK), 1)
        lo32 = lane % HEAD_DIM < HEAD_DIM // 2
        lo = lane < HEAD_DIM
        has_next = (pl.program_id(0) + 1 < nb).astype(F32)

        def unrope(dy):
            return dy * C + _rope_partner(dy * S, lo32)

        def fold(cur_ref, next_ref):
            b00, b01, b10, b11 = (cur_ref[:, k * BLOCK:(k + 1) * BLOCK] + has_next * next_ref[:, k * BLOCK:(k + 1) * BLOCK]
                                  for k in range(4))
            return jnp.where(lo, b00 + pltpu.roll(b01, HEAD_DIM, 1), pltpu.roll(b10, HEAD_DIM, 1) + b11)

        for m in range(SW_W // BLOCK):
            sl = slice(m * BLOCK, (m + 1) * BLOCK)
            o_ref[:, sl] = unrope(dq_ref[:, sl] * Q_SCALE).astype(BF16)
        o_ref[:, SW_W:SW_W + SB_W] = (dsq_ref[...] * Q_SCALE).astype(BF16)
        o_ref[:, SW_W + SB_W:SW_W + 2 * SB_W] = dsk_ref[...].astype(BF16)
        o_ref[:, SW_W + 2 * SB_W:SW_W + 3 * SB_W] = dsv_ref[...].astype(BF16)
        base = SW_W + 3 * SB_W
        o_ref[:, base:base + KV_W] = unrope(fold(dkc_ref, dkp_ref)).astype(BF16)
        o_ref[:, base + KV_W:base + 2 * KV_W] = fold(dvc_ref, dvp_ref).astype(BF16)

    row = lambda w: pl.BlockSpec((tm, w), lambda i: (i, 0))
    nxt = pl.BlockSpec((tm, SB_W), lambda i: (jnp.minimum(i + 1, nb - 1), 0))
    return pl.pallas_call(
        body, name="assemble_dqkv", grid=(nb,),
        in_specs=[row(SW_W), row(SB_W), row(SB_W), row(SB_W), nxt, row(SB_W), nxt, row(SB_W), row(BLOCK), row(BLOCK)],
        out_specs=row(QKV_W), out_shape=jax.ShapeDtypeStruct((L, QKV_W), BF16),
        compiler_params=_cparams("parallel"),
    )(dswq, dsbq, dsbk, dsbv, dkp, dkc, dvp, dvc, cos_t, sin_t)


def _norm_bwd(dxn, h0, dh1, gain):
    L = h0.shape[0]

    def body(dx_ref, h_ref, dh_ref, g_ref, gx_ref, dm_ref, dg_ref):
        i = pl.program_id(0)

        @pl.when(i == 0)
        def _():
            dg_ref[...] = jnp.zeros_like(dg_ref)

        h = h_ref[...]
        dxn_t = dx_ref[...]
        r = lax.rsqrt(jnp.mean(h * h, axis=-1, keepdims=True) + RMS_EPS)
        n = h * r
        dg_ref[...] += jnp.sum(dxn_t * n, axis=0, keepdims=True)
        dn = dxn_t * g_ref[...]
        dh0 = dh_ref[...] + r * (dn - n * jnp.mean(dn * n, axis=-1, keepdims=True))
        gx_ref[...] = dh0

        @pl.when(i == 0)
        def _():
            dm_ref[...] = dh0[PAD:BLOCK]

    row = pl.BlockSpec((BLOCK, D_MODEL), lambda i: (i, 0))
    return pl.pallas_call(
        body, name="norm_bwd", grid=(L // BLOCK,),
        in_specs=[row, row, row, pl.BlockSpec((1, D_MODEL), lambda i: (0, 0))],
        out_specs=[pl.BlockSpec((BLOCK, D_MODEL), lambda i: (jnp.maximum(i - 1, 0), 0)),
                   pl.BlockSpec((N_META, D_MODEL), lambda i: (0, 0)), pl.BlockSpec((1, D_MODEL), lambda i: (0, 0))],
        out_shape=[jax.ShapeDtypeStruct((L - BLOCK, D_MODEL), F32), jax.ShapeDtypeStruct((N_META, D_MODEL), F32),
                   jax.ShapeDtypeStruct((1, D_MODEL), F32)],
        compiler_params=_cparams("arbitrary"),
    )(dxn, h0, dh1, gain)


def _adamw(w, g, m, v, name):
    R, C = w.shape
    tr = _row_tile(R, 256) if R % 16 == 0 else R

    def body(w_ref, g_ref, m_ref, v_ref, d_ref, nm_ref, nv_ref):
        g = g_ref[...]
        m_new = ADAM_B1 * m_ref[...] + (1.0 - ADAM_B1) * g
        v_new = ADAM_B2 * v_ref[...] + (1.0 - ADAM_B2) * (g * g)
        m_hat = m_new / (1.0 - ADAM_B1 ** ADAM_STEP)
        v_hat = v_new / (1.0 - ADAM_B2 ** ADAM_STEP)
        d_ref[...] = -ADAM_LR * (m_hat / (jnp.sqrt(v_hat) + ADAM_EPS) + ADAM_WD * w_ref[...])
        nm_ref[...] = m_new
        nv_ref[...] = v_new

    spec = pl.BlockSpec((tr, C), lambda i: (i, 0))
    shp = jax.ShapeDtypeStruct((R, C), F32)
    return pl.pallas_call(
        body, name=name, grid=(R // tr,), in_specs=[spec] * 4, out_specs=[spec] * 3, out_shape=[shp] * 3,
        compiler_params=_cparams("parallel"),
    )(w, g, m, v)


def _sum_leading(a, name):
    n, R, C = a.shape
    tr = _row_tile(R, 160) if R % 16 == 0 else R

    def body(a_ref, o_ref):
        acc = a_ref[0]
        for k in range(1, n):
            acc = acc + a_ref[k]
        o_ref[...] = acc

    return pl.pallas_call(
        body, name=name, grid=(R // tr,), in_specs=[pl.BlockSpec((n, tr, C), lambda i: (0, i, 0))],
        out_specs=pl.BlockSpec((tr, C), lambda i: (i, 0)), out_shape=jax.ShapeDtypeStruct((R, C), a.dtype),
        compiler_params=_cparams("parallel"),
    )(a)


def _place():
    x, y, c = lax.axis_index("x"), lax.axis_index("y"), lax.axis_index("c")
    return x, y, c


def _all_gather(block, name):
    R, C = block.shape

    def body(x_ref, out_ref, send_sems, recv_sems, local_sem):
        x, y, c = _place()
        me, sibling = (x, y, c), (x, y, 1 - c)
        chips = [(1 - x, y), (x, 1 - y), (1 - x, 1 - y)]

        def slot(px, py, pc):
            return out_ref.at[4 * px + 2 * py + pc]

        def copy(k, blk, to, src=None):
            return pltpu.make_async_remote_copy(
                src_ref=slot(*blk) if src is None else src, dst_ref=slot(*blk),
                send_sem=send_sems.at[k], recv_sem=recv_sems.at[k], device_id=to, device_id_type=MESH)

        mine = pltpu.make_async_copy(x_ref, slot(*me), local_sem)
        mine.start()
        first = [copy(0, me, sibling, src=x_ref)]
        first += [copy(1 + j, me, (*chip, c), src=x_ref) for j, chip in enumerate(chips)]
        for cp in first:
            cp.start()
        passed = [copy(4 + j, (*chip, c), sibling) for j, chip in enumerate(chips)]
        for j, chip in enumerate(chips):
            copy(1 + j, (*chip, c), me).wait_recv()
            passed[j].start()
        copy(0, sibling, me).wait_recv()
        for j, chip in enumerate(chips):
            copy(4 + j, (*chip, 1 - c), me).wait_recv()
        for cp in first + passed:
            cp.wait_send()
        mine.wait()

    return pl.pallas_call(
        body, name=name, in_specs=[ANY], out_specs=ANY,
        out_shape=jax.ShapeDtypeStruct((N_DEV, R, C), block.dtype),
        scratch_shapes=[pltpu.SemaphoreType.DMA((7,)), pltpu.SemaphoreType.DMA((7,)), pltpu.SemaphoreType.DMA],
    )(block)


def _pair_exchange(g8):
    _, R, C = g8.shape

    def body(g_ref, out_ref, send_sems, recv_sems):
        x, y, c = _place()
        sibling = (x, y, 1 - c)
        sends = [pltpu.make_async_remote_copy(
            src_ref=g_ref.at[2 * s + (1 - c)], dst_ref=out_ref.at[s], send_sem=send_sems.at[s],
            recv_sem=recv_sems.at[s], device_id=sibling, device_id_type=MESH) for s in range(N_CHIP)]
        for cp in sends:
            cp.start()
        for s in range(N_CHIP):
            pltpu.make_async_remote_copy(
                src_ref=g_ref.at[s], dst_ref=out_ref.at[s], send_sem=send_sems.at[s],
                recv_sem=recv_sems.at[s], device_id=sibling, device_id_type=MESH).wait_recv()
        for cp in sends:
            cp.wait_send()

    return pl.pallas_call(
        body, name="grad_pair_exchange", in_specs=[ANY], out_specs=ANY,
        out_shape=jax.ShapeDtypeStruct((N_CHIP, R, C), g8.dtype),
        scratch_shapes=[pltpu.SemaphoreType.DMA((N_CHIP,)), pltpu.SemaphoreType.DMA((N_CHIP,))],
    )(g8)


def _add_selected(a, b, a_index, name, wire_copy=False):
    select, sel_fn = a_index
    bs = b if isinstance(b, (list, tuple)) else [b]
    m, R, C = bs[0].shape
    tr = _row_tile(R, 160)
    n_out = 2 if wire_copy else 1

    def body(sel_ref, a_ref, *refs):
        acc = a_ref[...]
        for r in refs[:len(bs)]:
            acc = acc + r[...].astype(F32)
        refs[len(bs)][...] = acc
        if wire_copy:
            refs[len(bs) + 1][...] = acc.astype(BF16)

    blk = lambda fn: pl.BlockSpec((None, tr, C), fn)
    same = blk(lambda s, i, sel: (s, i, 0))
    out = pl.pallas_call(
        body, name=name,
        grid_spec=pltpu.PrefetchScalarGridSpec(
            num_scalar_prefetch=1, grid=(m, R // tr),
            in_specs=[blk(lambda s, i, sel: (sel_fn(s, sel[0]), i, 0))] + [same] * len(bs),
            out_specs=[same] * n_out),
        out_shape=[jax.ShapeDtypeStruct((m, R, C), F32), jax.ShapeDtypeStruct((m, R, C), BF16)][:n_out],
        compiler_params=_cparams("parallel", "parallel"),
    )(jnp.reshape(select, (1,)).astype(jnp.int32), a, *bs)
    return out if wire_copy else out[0]


def _chip_scatter(p4):
    _, R, C = p4.shape

    def body(p_ref, out_x, out_y, out_xy, send_sems, recv_sems):
        x, y, c = _place()
        chips = [(1 - x, y), (x, 1 - y), (1 - x, 1 - y)]
        outs = [out_x, out_y, out_xy]
        sends = [pltpu.make_async_remote_copy(
            src_ref=p_ref.at[2 * cx + cy], dst_ref=outs[j].at[0], send_sem=send_sems.at[j],
            recv_sem=recv_sems.at[j], device_id=(cx, cy, c), device_id_type=MESH) for j, (cx, cy) in enumerate(chips)]
        for cp in sends:
            cp.start()
        for j, (cx, cy) in enumerate(chips):
            pltpu.make_async_remote_copy(
                src_ref=p_ref.at[0], dst_ref=outs[j].at[0], send_sem=send_sems.at[j],
                recv_sem=recv_sems.at[j], device_id=(cx, cy, c), device_id_type=MESH).wait_recv()
        for cp in sends:
            cp.wait_send()

    shp = jax.ShapeDtypeStruct((1, R, C), p4.dtype)
    return pl.pallas_call(
        body, name="grad_chip_scatter", in_specs=[ANY], out_specs=[ANY, ANY, ANY], out_shape=[shp, shp, shp],
        scratch_shapes=[pltpu.SemaphoreType.DMA((3,)), pltpu.SemaphoreType.DMA((3,))],
    )(p4)


def _pair_share(r):
    R, C = r.shape

    def body(r_ref, out_ref, send_sem, recv_sem):
        x, y, c = _place()
        copy = pltpu.make_async_remote_copy(src_ref=r_ref, dst_ref=out_ref, send_sem=send_sem, recv_sem=recv_sem,
                                            device_id=(x, y, 1 - c), device_id_type=MESH)
        copy.start()
        copy.wait()

    return pl.pallas_call(
        body, name="grad_pair_share", in_specs=[ANY], out_specs=ANY,
        out_shape=jax.ShapeDtypeStruct((R, C), r.dtype),
        scratch_shapes=[pltpu.SemaphoreType.DMA, pltpu.SemaphoreType.DMA],
    )(r)


def _gather_direct(block, name):
    R, C = block.shape

    def body(x_ref, out_ref, send_sems, recv_sems, local_sem):
        x, y, c = _place()
        me = 4 * x + 2 * y + c
        local = pltpu.make_async_copy(x_ref, out_ref.at[me], local_sem)
        local.start()
        peers = []
        for k in range(1, N_DEV):
            px, py, pc = x ^ (k >> 2), y ^ ((k >> 1) & 1), c ^ (k & 1)
            peers.append((k, (px, py, pc), 4 * px + 2 * py + pc))
        sends = [pltpu.make_async_remote_copy(
            src_ref=x_ref, dst_ref=out_ref.at[me], send_sem=send_sems.at[k - 1], recv_sem=recv_sems.at[k - 1],
            device_id=dev, device_id_type=MESH) for k, dev, _ in peers]
        for cp in sends:
            cp.start()
        for k, dev, idx in peers:
            pltpu.make_async_remote_copy(
                src_ref=x_ref, dst_ref=out_ref.at[idx], send_sem=send_sems.at[k - 1], recv_sem=recv_sems.at[k - 1],
                device_id=dev, device_id_type=MESH).wait_recv()
        for cp in sends:
            cp.wait_send()
        local.wait()

    return pl.pallas_call(
        body, name=name, in_specs=[ANY], out_specs=ANY,
        out_shape=jax.ShapeDtypeStruct((N_DEV, R, C), block.dtype),
        scratch_shapes=[pltpu.SemaphoreType.DMA((7,)), pltpu.SemaphoreType.DMA((7,)), pltpu.SemaphoreType.DMA],
    )(block)


W_IN_SHARD = IN_COLS // N_CHIP
ROWS_W_IN = (D_MODEL // 2) * W_IN_SHARD // D_MODEL
ROWS_BSB = (SB_W // 2) * (D_MODEL // N_CHIP) // D_MODEL
ROWS_SQ = D_MODEL // N_CHIP // 2
ROWS_BIG = ROWS_W_IN + ROWS_BSB + 2 * ROWS_SQ


def _half(a, c, axis):
    n = a.shape[axis] // 2
    return lax.dynamic_slice_in_dim(a, c * n, n, axis)


def _pack_shard_half(w_in, w_bsb, w_bsw, w_out, c):
    parts = [_half(w_in, c, 0).reshape(ROWS_W_IN, D_MODEL), _half(w_bsb, c, 0).reshape(ROWS_BSB, D_MODEL),
             _half(w_bsw, c, 0), _half(w_out, c, 0)]
    return jnp.concatenate(parts, axis=0)


def _unpack_full(blocks):
    b = blocks.reshape(N_CHIP, 2, ROWS_BIG, D_MODEL)
    o = 0
    w_in = b[:, :, o:o + ROWS_W_IN].reshape(N_CHIP, D_MODEL, W_IN_SHARD)
    o += ROWS_W_IN
    w_bsb = b[:, :, o:o + ROWS_BSB].reshape(N_CHIP, SB_W, D_MODEL // N_CHIP)
    o += ROWS_BSB
    w_bsw = b[:, :, o:o + ROWS_SQ].reshape(D_MODEL, D_MODEL)
    o += ROWS_SQ
    w_out = b[:, :, o:o + ROWS_SQ].reshape(D_MODEL, D_MODEL)
    w_in = jnp.transpose(w_in, (1, 0, 2)).reshape(D_MODEL, IN_COLS)
    w_bsb = jnp.transpose(w_bsb, (1, 0, 2)).reshape(SB_W, D_MODEL)
    return w_in, w_bsb, w_bsw, w_out


def _pack_grads(dw_in, dw_bsb, dw_bsw, dw_out):
    a = jnp.transpose(dw_in.reshape(2, D_MODEL // 2, N_CHIP, W_IN_SHARD), (2, 0, 1, 3)).reshape(N_CHIP, 2, ROWS_W_IN, D_MODEL)
    b = jnp.transpose(dw_bsb.reshape(2, SB_W // 2, N_CHIP, D_MODEL // N_CHIP), (2, 0, 1, 3)).reshape(N_CHIP, 2, ROWS_BSB, D_MODEL)
    c = dw_bsw.reshape(N_CHIP, 2, ROWS_SQ, D_MODEL)
    d = dw_out.reshape(N_CHIP, 2, ROWS_SQ, D_MODEL)
    return jnp.concatenate([a, b, c, d], axis=2).reshape(N_DEV, ROWS_BIG, D_MODEL)


def _unpack_shard(full2):
    o = 0
    w_in = full2[:, o:o + ROWS_W_IN].reshape(D_MODEL, W_IN_SHARD)
    o += ROWS_W_IN
    w_bsb = full2[:, o:o + ROWS_BSB].reshape(SB_W, D_MODEL // N_CHIP)
    o += ROWS_BSB
    w_bsw = full2[:, o:o + ROWS_SQ].reshape(D_MODEL // N_CHIP, D_MODEL)
    o += ROWS_SQ
    w_out = full2[:, o:o + ROWS_SQ].reshape(D_MODEL // N_CHIP, D_MODEL)
    return w_in, w_bsb, w_bsw, w_out


_QKV_FROM_IN = ((1536, 2560), (0, 512), (512, 1024), (1024, 1536), (2560, 2688), (2688, 2816))
_GATE_FROM_IN = ((3328, 4352), (4352, 5376), (5376, 6400), (2816, 3328))


def _split_w_in(w_in):
    qkv = jnp.concatenate([w_in[:, a:b] for a, b in _QKV_FROM_IN], axis=1)
    gate = jnp.concatenate([w_in[:, a:b] for a, b in _GATE_FROM_IN], axis=1)
    return qkv, gate


def _join_dw_in(dqkv, dgate):
    q = lambda a, b: dqkv[:, a:b]
    g = lambda a, b: dgate[:, a:b]
    return jnp.concatenate([q(1024, 1536), q(1536, 2048), q(2048, 2560), q(0, 1024), q(2560, 2688), q(2688, 2816),
                            g(3072, 3584), g(0, 1024), g(1024, 2048), g(2048, 3072)], axis=1)


def _rope_tables(L):
    half = HEAD_DIM // 2
    inv = ROPE_THETA ** (-jnp.arange(half, dtype=F32) / half)
    pos = (jnp.arange(L) - PAD).astype(F32)
    ang = pos[:, None] * inv[None, :]
    cos, sin = jnp.cos(ang), jnp.sin(ang)
    return jnp.tile(jnp.concatenate([cos, cos], axis=1), (1, 2)), jnp.tile(jnp.concatenate([-sin, sin], axis=1), (1, 2))


def kernel(x, meta_tokens, norm_gain, w_in, w_branch_sb, w_branch_swa, w_out, attn_sinks, final_norm_gain, loss_target, m_meta_tokens, m_norm_gain, m_w_in, m_w_branch_sb, m_w_branch_swa, m_w_out, m_attn_sinks, m_final_norm_gain, v_meta_tokens, v_norm_gain, v_w_in, v_w_branch_sb, v_w_branch_swa, v_w_out, v_attn_sinks, v_final_norm_gain):
    xi, yi, ci = _place()
    chip = 2 * xi + yi
    seq = x.shape[1]
    L = seq + BLOCK

    wpack = _pack_shard_half(w_in[0], w_branch_sb[0], w_branch_swa[0], w_out[0], ci).astype(BF16)
    w_in_f, w_bsb_f, w_bsw_f, w_out_f = _unpack_full(_all_gather(wpack, "weight_all_gather"))
    mg = _all_gather(_half(meta_tokens, ci, 0), "meta_all_gather").reshape(N_CHIP, 2, N_META // 2, D_MODEL // N_CHIP)
    meta_full = jnp.transpose(mg, (1, 2, 0, 3)).reshape(N_META, D_MODEL)
    w_qkv, w_gate = _split_w_in(w_in_f)

    h0 = jnp.concatenate([jnp.zeros((PAD, D_MODEL), F32), meta_full, x[0]], axis=0)
    cos_t, sin_t = _rope_tables(L)
    xn = _norm_fwd(h0, norm_gain)
    qkv = _mm([(xn, w_qkv)], F32, "in_proj_qkv", tn_pref=1408)
    gate = _mm([(xn, w_gate)], F32, "in_proj_gate")
    swq, sbq, kcat, vcat, swk4, swv4 = _prep_qkv(qkv, cos_t, sin_t)
    o_sb = _sb_fwd(sbq, kcat, vcat)
    o_sw = _swa_fwd(swq, swk4, swv4, attn_sinks)
    h1, u_sb, u_sw, merged, y_sb, y_sw = _post_fwd(o_sb, o_sw, gate, h0, w_bsb_f, w_bsw_f, w_out_f)

    dh1, loss_part, dgain2 = _loss_bwd(h1, loss_target[0], final_norm_gain.reshape(1, D_MODEL))
    dy_sb, dy_sw, do_sb, do_sw, dgate = _post_bwd(dh1, gate, y_sb, y_sw, o_sb, o_sw, w_out_f.T, w_bsb_f.T, w_bsw_f.T)
    dsbq, dsbk, dsbv = _sb_bwd(sbq, kcat, vcat, do_sb)
    dswq, dkp, dkc, dvp, dvc, dsink = _swa_bwd(swq, swk4, swv4, attn_sinks, do_sw)
    dqkv = _assemble_dqkv(dswq, dsbq, dsbk, dsbv, dkp, dkc, dvp, dvc, cos_t, sin_t)
    dxn = _mm([(dqkv, w_qkv.T), (dgate, w_gate.T)], F32, "dxn", tm_pref=320, tn_pref=1024)
    grad_x, dmeta, dgain1 = _norm_bwd(dxn, h0, dh1, norm_gain)
    dw_qkv = _mm_tn(xn, dqkv, "dw_qkv", tn_pref=1408)
    dw_gate = _mm_tn(xn, dgate, "dw_gate")
    dw_bsb = _mm_tn(u_sb, dy_sb, "dw_branch_sb", tn_pref=1024)
    dw_bsw = _mm_tn(u_sw, dy_sw, "dw_branch_swa", tn_pref=1024)
    dw_out = _mm_tn(merged, dh1, "dw_out", tn_pref=1024)

    g8 = _pack_grads(_join_dw_in(dw_qkv, dw_gate), dw_bsb, dw_bsw, dw_out)
    p4, p4_wire = _add_selected(g8, _pair_exchange(g8), (ci, lambda s, c: 2 * s + c), "grad_pair_sum", wire_copy=True)
    r_half = _add_selected(p4, list(_chip_scatter(p4_wire)), (chip, lambda s, own: own), "grad_chip_sum")[0]
    r_sib = _pair_share(r_half)
    full2 = jnp.where(ci == 0, jnp.stack([r_half, r_sib]), jnp.stack([r_sib, r_half]))
    g_in, g_bsb, g_bsw, g_out = _unpack_shard(full2)

    small = jnp.concatenate([dmeta, dgain1, dgain2, jnp.pad(dsink[0:1], ((0, 0), (0, D_MODEL - BLOCK))),
                             jnp.pad(loss_part, ((0, 0), (0, D_MODEL - BLOCK))),
                             jnp.zeros((32 - N_META - 4, D_MODEL), F32)], axis=0)
    tot = _sum_leading(_gather_direct(small, "small_all_gather"), "small_sum")
    g_meta = lax.dynamic_slice_in_dim(tot[0:N_META], chip * (D_MODEL // N_CHIP), D_MODEL // N_CHIP, 1)
    g_gain1 = tot[N_META:N_META + 1]
    g_gain2 = tot[N_META + 1]
    g_sinks = tot[N_META + 2:N_META + 3, 0:16]
    loss = tot[N_META + 3, 0]

    def upd(w, g, m, v, name):
        shape = w.shape
        as2d = lambda a: a.reshape(-1, shape[-1])
        d, nm, nv = _adamw(as2d(w), as2d(g), as2d(m), as2d(v), name)
        return g.reshape(shape), d.reshape(shape), nm.reshape(shape), nv.reshape(shape)

    res = [
        upd(meta_tokens, g_meta, m_meta_tokens, v_meta_tokens, "adamw_meta"),
        upd(norm_gain, g_gain1, m_norm_gain, v_norm_gain, "adamw_norm_gain"),
        upd(w_in, g_in, m_w_in, v_w_in, "adamw_w_in"),
        upd(w_branch_sb, g_bsb, m_w_branch_sb, v_w_branch_sb, "adamw_w_branch_sb"),
        upd(w_branch_swa, g_bsw, m_w_branch_swa, v_w_branch_swa, "adamw_w_branch_swa"),
        upd(w_out, g_out, m_w_out, v_w_out, "adamw_w_out"),
        upd(attn_sinks, g_sinks, m_attn_sinks, v_attn_sinks, "adamw_attn_sinks"),
        upd(final_norm_gain, g_gain2, m_final_norm_gain, v_final_norm_gain, "adamw_final_norm_gain"),
    ]
    grads, deltas, new_m, new_v = zip(*res)
    return (loss, grad_x.reshape(1, seq, D_MODEL), *grads, *deltas, *new_m, *new_v)
```

```python
import functools

import jax
import jax.numpy as jnp
from jax import lax
from jax.experimental import pallas as pl
from jax.experimental.pallas import tpu as pltpu

F32 = jnp.float32
BF16 = jnp.bfloat16

D_MODEL = 1024
BLOCK = 128
N_META = 16
PAD = BLOCK - N_META
HEAD_DIM = 64
SB_W = 512
SW_W = 1024
KV_W = 128
QKV_W = SW_W + 3 * SB_W + 2 * KV_W
GATE_W = 2 * D_MODEL + SW_W + SB_W
IN_COLS = QKV_W + GATE_W
ROPE_THETA = 10000.0
RMS_EPS = 1e-6
Q_SCALE = HEAD_DIM ** -0.5

ADAM_LR = 0.001
ADAM_B1 = 0.9
ADAM_B2 = 0.999
ADAM_EPS = 1e-08
ADAM_WD = 0.01
ADAM_STEP = 10

N_DEV = 8
N_CHIP = 4
MESH = pl.DeviceIdType.MESH
VMEM_LIMIT = 56 * 1024 * 1024
ANY = pl.BlockSpec(memory_space=pl.ANY)


def _cparams(*sem):
    return pltpu.CompilerParams(dimension_semantics=sem, vmem_limit_bytes=VMEM_LIMIT)


def _row_tile(n, pref):
    best = None
    for t in range(16, pref + 1, 16):
        if n % t == 0:
            best = t
    assert best is not None, (n, pref)
    return best


def _col_tile(n, pref):
    best = None
    for t in range(128, pref + 1, 128):
        if n % t == 0:
            best = t
    assert best is not None, (n, pref)
    return best


def _dot(a, b):
    return jnp.dot(a, b, preferred_element_type=F32)


def _dot_nt(a, b):
    return lax.dot_general(a, b, (((1,), (1,)), ((), ())), preferred_element_type=F32)


def _dot_tn(a, b):
    return lax.dot_general(a, b, (((0,), (0,)), ((), ())), preferred_element_type=F32)


def _lane_lo(shape):
    return lax.broadcasted_iota(jnp.int32, shape, len(shape) - 1) % BLOCK < HEAD_DIM


def _mm(pairs, out_dtype, name, tm_pref=640, tn_pref=1792):
    M = pairs[0][0].shape[0]
    N = pairs[0][1].shape[1]
    tm = _row_tile(M, tm_pref)
    tn = _col_tile(N, tn_pref)
    n_pairs = len(pairs)

    def body(*refs):
        o_ref = refs[-1]
        acc = None
        for p in range(n_pairs):
            d = _dot(refs[2 * p][...].astype(BF16), refs[2 * p + 1][...])
            acc = d if acc is None else acc + d
        o_ref[...] = acc.astype(out_dtype)

    in_specs, args = [], []
    for a, b in pairs:
        k = a.shape[1]
        in_specs += [pl.BlockSpec((tm, k), lambda n, m: (m, 0)), pl.BlockSpec((k, tn), lambda n, m: (0, n))]
        args += [a, b]
    return pl.pallas_call(
        body, name=name, grid=(N // tn, M // tm), in_specs=in_specs,
        out_specs=pl.BlockSpec((tm, tn), lambda n, m: (m, n)),
        out_shape=jax.ShapeDtypeStruct((M, N), out_dtype),
        compiler_params=_cparams("parallel", "arbitrary"),
    )(*args)


def _mm_tn(a, b, name, tn_pref=1792, tl_pref=640):
    L, M = a.shape
    N = b.shape[1]
    tn = _col_tile(N, tn_pref)
    tl = _row_tile(L, tl_pref)

    def body(a_ref, b_ref, o_ref):
        @pl.when(pl.program_id(1) == 0)
        def _():
            o_ref[...] = jnp.zeros_like(o_ref)
        o_ref[...] += _dot_tn(a_ref[...].astype(BF16), b_ref[...].astype(BF16))

    return pl.pallas_call(
        body, name=name, grid=(N // tn, L // tl),
        in_specs=[pl.BlockSpec((tl, M), lambda n, l: (l, 0)), pl.BlockSpec((tl, tn), lambda n, l: (l, n))],
        out_specs=pl.BlockSpec((M, tn), lambda n, l: (0, n)),
        out_shape=jax.ShapeDtypeStruct((M, N), F32),
        compiler_params=_cparams("parallel", "arbitrary"),
    )(a, b)


def _norm_fwd(h0, gain):
    L = h0.shape[0]
    tm = _row_tile(L, 640)

    def body(h_ref, g_ref, o_ref):
        h = h_ref[...]
        r = lax.rsqrt(jnp.mean(h * h, axis=-1, keepdims=True) + RMS_EPS)
        o_ref[...] = ((h * r) * g_ref[...]).astype(BF16)

    return pl.pallas_call(
        body, name="norm_fwd", grid=(L // tm,),
        in_specs=[pl.BlockSpec((tm, D_MODEL), lambda i: (i, 0)), pl.BlockSpec((1, D_MODEL), lambda i: (0, 0))],
        out_specs=pl.BlockSpec((tm, D_MODEL), lambda i: (i, 0)),
        out_shape=jax.ShapeDtypeStruct((L, D_MODEL), BF16),
        compiler_params=_cparams("parallel"),
    )(h0, gain)


def _rope_partner(x, lo32):
    return jnp.where(lo32, pltpu.roll(x, 96, 1), pltpu.roll(x, 32, 1))


def _prep_qkv(qkv, cos_t, sin_t):
    L = qkv.shape[0]
    tm = BLOCK * max(d for d in range(1, 6) if (L // BLOCK) % d == 0)
    n_steps = L // tm
    assert L // BLOCK + tm // BLOCK >= _padded_blocks(L)

    def body(x_ref, c_ref, s_ref, swq, sbq, kcat, vcat, swk4, swv4):
        C = c_ref[...]
        S = s_ref[...]
        lane = lax.broadcasted_iota(jnp.int32, (tm, BLOCK), 1)
        lo32 = lane % HEAD_DIM < HEAD_DIM // 2
        lo = lane < HEAD_DIM

        def rope(x):
            return x * C + _rope_partner(x, lo32) * S

        for m in range(SW_W // BLOCK):
            sl = slice(m * BLOCK, (m + 1) * BLOCK)
            swq[:, sl] = (rope(x_ref[:, sl]) * Q_SCALE).astype(BF16)
        for m in range(SB_W // BLOCK):
            sl = slice(m * BLOCK, (m + 1) * BLOCK)
            sbq[:, sl] = (x_ref[:, SW_W + m * BLOCK:SW_W + (m + 1) * BLOCK] * Q_SCALE).astype(BF16)
            k = x_ref[:, SW_W + SB_W + m * BLOCK:SW_W + SB_W + (m + 1) * BLOCK]
            v = x_ref[:, SW_W + 2 * SB_W + m * BLOCK:SW_W + 2 * SB_W + (m + 1) * BLOCK]
            for src, dst in ((k, kcat), (v, vcat)):
                even = jnp.where(lo, src, 0.0).astype(BF16)
                odd = jnp.where(lo, 0.0, src).astype(BF16)
                for b in range(tm // BLOCK):
                    dst[2 * b * BLOCK:(2 * b + 1) * BLOCK, sl] = even[b * BLOCK:(b + 1) * BLOCK]
                    dst[(2 * b + 1) * BLOCK:(2 * b + 2) * BLOCK, sl] = odd[b * BLOCK:(b + 1) * BLOCK]
        base = SW_W + 3 * SB_W
        kx = rope(x_ref[:, base:base + KV_W])
        vx = x_ref[:, base + KV_W:base + 2 * KV_W]
        for src, dst in ((kx, swk4), (vx, swv4)):
            sw = pltpu.roll(src, HEAD_DIM, 1)
            dst[:, 0:128] = jnp.where(lo, src, 0.0).astype(BF16)
            dst[:, 128:256] = jnp.where(lo, 0.0, sw).astype(BF16)
            dst[:, 256:384] = jnp.where(lo, sw, 0.0).astype(BF16)
            dst[:, 384:512] = jnp.where(lo, 0.0, src).astype(BF16)

        @pl.when(pl.program_id(0) == n_steps)
        def _():
            kcat[...] = jnp.zeros_like(kcat)
            vcat[...] = jnp.zeros_like(vcat)

    row = lambda w: pl.BlockSpec((tm, w), lambda i: (jnp.minimum(i, n_steps - 1), 0))
    row2 = pl.BlockSpec((2 * tm, SB_W), lambda i: (i, 0))
    shp = lambda r, w: jax.ShapeDtypeStruct((r, w), BF16)
    return pl.pallas_call(
        body, name="prep_qkv", grid=(n_steps + 1,),
        in_specs=[row(QKV_W), row(BLOCK), row(BLOCK)],
        out_specs=[row(SW_W), row(SB_W), row2, row2, row(SB_W), row(SB_W)],
        out_shape=[shp(L, SW_W), shp(L, SB_W), shp(2 * (L + tm), SB_W), shp(2 * (L + tm), SB_W), shp(L, SB_W),
                   shp(L, SB_W)],
        compiler_params=_cparams("arbitrary"),
    )(qkv, cos_t, sin_t)


WIDE = 2 * BLOCK
GROUP = 4
SUPER = GROUP * WIDE
QROWS = 2 * BLOCK
QROWS_FWD = 4 * BLOCK


def _padded_blocks(L):
    return -(-(L // BLOCK) // GROUP) * GROUP


def _cumsum_matrices():
    r = lax.broadcasted_iota(jnp.int32, (WIDE, WIDE), 0)
    c = lax.broadcasted_iota(jnp.int32, (WIDE, WIDE), 1)
    same = (r < BLOCK) == (c < BLOCK)
    rev = jnp.where(same & (r >= c), 1.0, 0.0).astype(BF16)
    fwd = jnp.where(same & (r <= c), 1.0, 0.0).astype(BF16)
    return rev, fwd


def _head_totals(c, col):
    half = lax.broadcasted_iota(jnp.int32, c.shape, 1) < BLOCK
    return jnp.where(half, c[:, col:col + 1], c[:, BLOCK + col:BLOCK + col + 1])


def _chunks(x):
    return [x[:, t * WIDE:(t + 1) * WIDE] for t in range(GROUP)]


def _sb_diag_mask(i, rows=BLOCK):
    t = i * rows + lax.broadcasted_iota(jnp.int32, (rows, 1), 0)
    lane = lax.broadcasted_iota(jnp.int32, (1, SUPER), 1)
    s = ((i * (rows // BLOCK) // GROUP) * GROUP + lane // WIDE) * BLOCK + lane % BLOCK
    s = jnp.where(s >= PAD, s, jnp.int32(2 ** 30))
    return s < t


def _sb_logits(q, kc, mask):
    z = _dot_nt(q, kc)
    sp = jnp.log(1.0 + jnp.exp(-jnp.abs(z)))
    lb = jnp.minimum(z, 0.0) - sp
    l1m = lb - z
    if mask is not None:
        l1m = jnp.where(mask, l1m, 0.0)
    hi = l1m.astype(BF16)
    lo = (l1m - hi.astype(F32)).astype(BF16)
    return z, hi, lo, lb


def _sb_suffix(z, hi, lo, U_rev):
    cs = [_dot(h, U_rev) + _dot(l, U_rev) for h, l in zip(_chunks(hi), _chunks(lo))]
    ys, run = [None] * GROUP, None
    for t in reversed(range(GROUP)):
        y = z[:, t * WIDE:(t + 1) * WIDE] + cs[t]
        ys[t] = y if run is None else y + run
        tot = _head_totals(cs[t], 0)
        run = tot if run is None else run + tot
    return jnp.concatenate(ys, axis=1), run


def _super_rows(J):
    return pl.ds(pl.multiple_of(J * SUPER, SUPER), SUPER)


def _sb_fwd(sbq, kcat, vcat):
    L = sbq.shape[0]
    nq = -(-L // QROWS_FWD)
    sbq = jnp.pad(sbq, ((0, nq * QROWS_FWD - L), (0, 0)))

    def body(q_ref, k_ref, v_ref, o_ref):
        i = pl.program_id(1)
        jd = i * (QROWS_FWD // BLOCK) // GROUP
        q = q_ref[...]
        U_rev, _ = _cumsum_matrices()

        def scores(J, mask=None):
            z, hi, lo, _ = _sb_logits(q, k_ref[_super_rows(J), :], mask)
            return _sb_suffix(z, hi, lo, U_rev)

        def weighted(J, y, later, mask=None):
            w = jnp.exp(y + jnp.concatenate([later] * GROUP, axis=1))
            if mask is not None:
                w = jnp.where(mask, w, 0.0)
            return _dot(w.astype(BF16), v_ref[_super_rows(J), :])

        mask = _sb_diag_mask(i, QROWS_FWD)
        y, tot = scores(jd, mask)
        acc = weighted(jd, y, jnp.zeros((QROWS_FWD, WIDE), F32), mask)
        later = tot
        y, tot = scores(jnp.maximum(jd - 1, 0))

        def step(jj, st):
            acc, later, y, tot = st
            J = jd - 1 - jj
            acc = acc + weighted(J, y, later)
            y_next, tot_next = scores(J - 1)
            return acc, later + tot, y_next, tot_next

        acc, later, y, tot = lax.fori_loop(0, jnp.maximum(jd - 1, 0), step, (acc, later, y, tot))
        last = weighted(0, y, later)
        o_ref[...] = acc + jnp.where(jd >= 1, last, 0.0)

    qspec = pl.BlockSpec((QROWS_FWD, BLOCK), lambda p, i: (i, p))
    kvspec = pl.BlockSpec((kcat.shape[0], BLOCK), lambda p, i: (0, p))
    return pl.pallas_call(
        body, name="sb_fwd", grid=(SB_W // BLOCK, nq),
        in_specs=[qspec, kvspec, kvspec], out_specs=qspec,
        out_shape=jax.ShapeDtypeStruct((nq * QROWS_FWD, SB_W), F32),
        compiler_params=_cparams("parallel", "arbitrary"),
    )(sbq, kcat, vcat)


def _sb_bwd(sbq, kcat, vcat, do_sb):
    L = sbq.shape[0]
    nq = -(-L // QROWS)
    sbq = jnp.pad(sbq, ((0, nq * QROWS - L), (0, 0)))
    do_sb = jnp.pad(do_sb, ((0, nq * QROWS - L), (0, 0)))
    n_super = _padded_blocks(L) // GROUP
    LP = _padded_blocks(L) * BLOCK

    def body(q_ref, k_hbm, v_hbm, do_ref, dq_ref, dk_hbm, dv_hbm, k_ref, v_ref, dk_ref, dv_ref, g_scr, beta_scr):
        i = pl.program_id(1)
        jd = i * (QROWS // BLOCK) // GROUP
        pair_cols = pl.ds(pl.multiple_of(pl.program_id(0) * BLOCK, BLOCK), BLOCK)

        @pl.when(i == 0)
        def _():
            pltpu.sync_copy(k_hbm.at[:, pair_cols], k_ref)
            pltpu.sync_copy(v_hbm.at[:, pair_cols], v_ref)
            dk_ref[...] = jnp.zeros_like(dk_ref)
            dv_ref[...] = jnp.zeros_like(dv_ref)

        lo = _lane_lo((QROWS, BLOCK))
        q = q_ref[...]
        qf = q.astype(F32)
        do = do_ref[...]
        do_b = do.astype(BF16)
        q_stack = jnp.concatenate([jnp.where(lo, qf, 0.0), jnp.where(lo, 0.0, qf)], axis=0).astype(BF16)
        do_stack = jnp.concatenate([jnp.where(lo, do, 0.0), jnp.where(lo, 0.0, do)], axis=0).astype(BF16)
        U_rev, U_fwd = _cumsum_matrices()
        mask = _sb_diag_mask(i, QROWS)

        def stack(x):
            return jnp.concatenate([x[:, :BLOCK], x[:, BLOCK:]], axis=0)

        def add_key_rows(ref, J, x, other):
            for t, xt in enumerate(_chunks(x)):
                rows = pl.ds(pl.multiple_of((J * GROUP + t) * BLOCK, BLOCK), BLOCK)
                ref[rows, :] += _dot_tn(stack(xt), other)

        def scores(J, mask=None):
            z, hi, lo, lb = _sb_logits(q, k_ref[_super_rows(J), :], mask)
            beta_scr[J] = jnp.exp(lb).astype(BF16)
            return _sb_suffix(z, hi, lo, U_rev)

        def weights(J, y, later, mask=None):
            w = jnp.exp(y + jnp.concatenate([later] * GROUP, axis=1))
            if mask is not None:
                w = jnp.where(mask, w, 0.0)
            g_scr[J] = (_dot_nt(do_b, v_ref[_super_rows(J), :]) * w).astype(BF16)
            add_key_rows(dv_ref, J, w.astype(BF16), do_stack)

        y, tot = scores(jd, mask)
        weights(jd, y, jnp.zeros((QROWS, WIDE), F32), mask)
        later = tot
        y, tot = scores(jnp.maximum(jd - 1, 0))

        def down(jj, st):
            later, y, tot = st
            J = jd - 1 - jj
            weights(J, y, later)
            y_next, tot_next = scores(J - 1)
            return later + tot, y_next, tot_next

        later, y, tot = lax.fori_loop(0, jnp.maximum(jd - 1, 0), down, (later, y, tot))

        @pl.when(jd >= 1)
        def _():
            weights(0, y, later)

        def prefix(J):
            excl, run = [], None
            for g in _chunks(g_scr[J]):
                upto = _dot(g, U_fwd)
                e = upto - g.astype(F32)
                excl.append(e if run is None else e + run)
                tot = _head_totals(upto, BLOCK - 1)
                run = tot if run is None else run + tot
            return jnp.concatenate(excl, axis=1), run

        def grads(J, excl, earlier, mask=None):
            g = g_scr[J].astype(F32)
            beta = beta_scr[J].astype(F32)
            dz = g * (1.0 - beta) - (excl + jnp.concatenate([earlier] * GROUP, axis=1)) * beta
            if mask is not None:
                dz = jnp.where(mask, dz, 0.0)
            dz = dz.astype(BF16)
            add_key_rows(dk_ref, J, dz, q_stack)
            return _dot(dz, k_ref[_super_rows(J), :])

        excl, tot_g = prefix(0)

        def up(J, st):
            dq, earlier, excl, tot_g = st
            dq = dq + grads(J, excl, earlier)
            excl_next, tot_next = prefix(J + 1)
            return dq, earlier + tot_g, excl_next, tot_next

        zero = jnp.zeros((QROWS, WIDE), F32)
        dq, earlier, excl, tot_g = lax.fori_loop(0, jd, up, (jnp.zeros((QROWS, BLOCK), F32), zero, excl, tot_g))
        dq_ref[...] = dq + grads(jd, excl, earlier, mask)

        @pl.when(i == nq - 1)
        def _():
            pltpu.sync_copy(dk_ref, dk_hbm.at[:, pair_cols])
            pltpu.sync_copy(dv_ref, dv_hbm.at[:, pair_cols])

    qspec = pl.BlockSpec((QROWS, BLOCK), lambda p, i: (i, p))
    acc_shape = jax.ShapeDtypeStruct((LP, SB_W), F32)
    return pl.pallas_call(
        body, name="sb_bwd", grid=(SB_W // BLOCK, nq),
        in_specs=[qspec, ANY, ANY, qspec],
        out_specs=[qspec, ANY, ANY],
        out_shape=[jax.ShapeDtypeStruct((nq * QROWS, SB_W), F32), acc_shape, acc_shape],
        scratch_shapes=[pltpu.VMEM((kcat.shape[0], BLOCK), BF16), pltpu.VMEM((kcat.shape[0], BLOCK), BF16),
                        pltpu.VMEM((LP, BLOCK), F32), pltpu.VMEM((LP, BLOCK), F32),
                        pltpu.VMEM((n_super, QROWS, SUPER), BF16), pltpu.VMEM((n_super, QROWS, SUPER), BF16)],
        compiler_params=_cparams("parallel", "arbitrary"),
    )(sbq, kcat, vcat, do_sb)


def _swa_mask(i):
    t = lax.broadcasted_iota(jnp.int32, (BLOCK, 2 * BLOCK), 0)
    s = lax.broadcasted_iota(jnp.int32, (BLOCK, 2 * BLOCK), 1)
    diff = BLOCK + t - s
    return (diff >= 0) & (diff < BLOCK) & ((i - 1) * BLOCK + s >= PAD)


def _swa_probs(q, kk, mask, sink):
    sc = jnp.where(mask, _dot_nt(q, kk), -jnp.inf)
    mx = jnp.maximum(jnp.max(sc, axis=1, keepdims=True), sink)
    e = jnp.exp(sc - mx)
    es = jnp.exp(sink - mx)
    inv = 1.0 / (jnp.sum(e, axis=1, keepdims=True) + es)
    return e * inv, es * inv


def _swa_specs(L):
    prev = lambda w: pl.BlockSpec((BLOCK, w), lambda i: (jnp.maximum(i - 1, 0), 0))
    cur = lambda w: pl.BlockSpec((BLOCK, w), lambda i: (i, 0))
    sink = pl.BlockSpec(memory_space=pltpu.SMEM)
    return [cur(SW_W), prev(SB_W), cur(SB_W), prev(SB_W), cur(SB_W), sink]


def _swa_fwd(swq, swk4, swv4, sinks):
    L = swq.shape[0]

    def body(q_ref, kp_ref, kc_ref, vp_ref, vc_ref, sink_ref, o_ref):
        mask = _swa_mask(pl.program_id(0))
        k2 = jnp.concatenate([kp_ref[...], kc_ref[...]], axis=0)
        v2 = jnp.concatenate([vp_ref[...], vc_ref[...]], axis=0)
        for m in range(SW_W // BLOCK):
            g = m // 4
            sl = slice(m * BLOCK, (m + 1) * BLOCK)
            q = q_ref[:, sl]
            acc = jnp.zeros((BLOCK, BLOCK), F32)
            for par in range(2):
                kv = slice((2 * g + par) * BLOCK, (2 * g + par + 1) * BLOCK)
                p, _ = _swa_probs(q, k2[:, kv], mask, sink_ref[0, 2 * m + par])
                acc = acc + _dot(p.astype(BF16), v2[:, kv])
            o_ref[:, sl] = acc

    return pl.pallas_call(
        body, name="swa_fwd", grid=(L // BLOCK,), in_specs=_swa_specs(L),
        out_specs=pl.BlockSpec((BLOCK, SW_W), lambda i: (i, 0)),
        out_shape=jax.ShapeDtypeStruct((L, SW_W), F32),
        compiler_params=_cparams("parallel"),
    )(swq, swk4, swk4, swv4, swv4, sinks)


def _swa_bwd(swq, swk4, swv4, sinks, do_sw):
    L = swq.shape[0]

    def body(q_ref, kp_ref, kc_ref, vp_ref, vc_ref, sink_ref, do_ref, dq_ref, dkp_ref, dkc_ref, dvp_ref, dvc_ref, ds_ref):
        i = pl.program_id(0)

        @pl.when(i == 0)
        def _():
            ds_ref[...] = jnp.zeros_like(ds_ref)

        mask = _swa_mask(i)
        k2 = jnp.concatenate([kp_ref[...], kc_ref[...]], axis=0)
        v2 = jnp.concatenate([vp_ref[...], vc_ref[...]], axis=0)
        lane = lax.broadcasted_iota(jnp.int32, (8, BLOCK), 1)
        dsink = jnp.zeros((8, BLOCK), F32)
        for g in range(2):
            dkk = [jnp.zeros((2 * BLOCK, BLOCK), F32) for _ in range(2)]
            dvv = [jnp.zeros((2 * BLOCK, BLOCK), F32) for _ in range(2)]
            for m in range(4 * g, 4 * g + 4):
                sl = slice(m * BLOCK, (m + 1) * BLOCK)
                q = q_ref[:, sl]
                do_b = do_ref[:, sl].astype(BF16)
                dq = jnp.zeros((BLOCK, BLOCK), F32)
                for par in range(2):
                    kv = slice((2 * g + par) * BLOCK, (2 * g + par + 1) * BLOCK)
                    kk = k2[:, kv]
                    p, ps = _swa_probs(q, kk, mask, sink_ref[0, 2 * m + par])
                    dp = _dot_nt(do_b, v2[:, kv])
                    dd = jnp.sum(p * dp, axis=1, keepdims=True)
                    ds = (p * (dp - dd)).astype(BF16)
                    dsink = dsink - jnp.where(lane == 2 * m + par, jnp.sum(ps * dd), 0.0)
                    dq = dq + _dot(ds, kk)
                    dkk[par] = dkk[par] + _dot_tn(ds, q)
                    dvv[par] = dvv[par] + _dot_tn(p.astype(BF16), do_b)
                dq_ref[:, sl] = dq
            for par in range(2):
                kv = slice((2 * g + par) * BLOCK, (2 * g + par + 1) * BLOCK)
                dkp_ref[:, kv] = dkk[par][0:BLOCK]
                dkc_ref[:, kv] = dkk[par][BLOCK:2 * BLOCK]
                dvp_ref[:, kv] = dvv[par][0:BLOCK]
                dvc_ref[:, kv] = dvv[par][BLOCK:2 * BLOCK]
        ds_ref[...] += dsink

    part = pl.BlockSpec((BLOCK, SB_W), lambda i: (i, 0))
    row = pl.BlockSpec((BLOCK, SW_W), lambda i: (i, 0))
    pshape = jax.ShapeDtypeStruct((L, SB_W), F32)
    return pl.pallas_call(
        body, name="swa_bwd", grid=(L // BLOCK,), in_specs=_swa_specs(L) + [row],
        out_specs=[row, part, part, part, part, pl.BlockSpec((8, BLOCK), lambda i: (0, 0))],
        out_shape=[jax.ShapeDtypeStruct((L, SW_W), F32), pshape, pshape, pshape, pshape,
                   jax.ShapeDtypeStruct((8, BLOCK), F32)],
        compiler_params=_cparams("arbitrary"),
    )(swq, swk4, swk4, swv4, swv4, sinks, do_sw)


def _gate_specs(tm):
    return [pl.BlockSpec((tm, SW_W), lambda i: (i, 0)), pl.BlockSpec((tm, D_MODEL), lambda i: (i, 1)),
            pl.BlockSpec((tm, D_MODEL), lambda i: (i, 2)), pl.BlockSpec((tm, SB_W), lambda i: (i, 6))]


def _post_fwd(o_sb, o_sw, gate, h0, w_bsb, w_bsw, w_out):
    L = h0.shape[0]
    tm = _row_tile(L, 320)

    def body(osb, osw, swz, gsb, gsw, sbz, h0_ref, wb1, wb2, wo, h1, usb, usw, mrg, ysb, ysw):
        z1 = sbz[...]
        z2 = swz[...]
        u1 = (osb[...] * (z1 * jax.nn.sigmoid(z1))).astype(BF16)
        u2 = (osw[...] * (z2 * jax.nn.sigmoid(z2))).astype(BF16)
        y1 = _dot(u1, wb1[...])
        y2 = _dot(u2, wb2[...])
        merged = (jax.nn.sigmoid(gsb[...]) * y1 + jax.nn.sigmoid(gsw[...]) * y2).astype(BF16)
        h1[...] = h0_ref[...] + _dot(merged, wo[...])
        usb[...] = u1
        usw[...] = u2
        mrg[...] = merged
        ysb[...] = y1
        ysw[...] = y2

    row = lambda w: pl.BlockSpec((tm, w), lambda i: (i, 0))
    whole = lambda a: pl.BlockSpec(a.shape, lambda i: (0, 0))
    shp = lambda w, dt: jax.ShapeDtypeStruct((L, w), dt)
    return pl.pallas_call(
        body, name="post_fwd", grid=(L // tm,),
        in_specs=[row(SB_W), row(SW_W)] + _gate_specs(tm) + [row(D_MODEL), whole(w_bsb), whole(w_bsw), whole(w_out)],
        out_specs=[row(D_MODEL), row(SB_W), row(SW_W), row(D_MODEL), row(D_MODEL), row(D_MODEL)],
        out_shape=[shp(D_MODEL, F32), shp(SB_W, BF16), shp(SW_W, BF16), shp(D_MODEL, BF16),
                   shp(D_MODEL, F32), shp(D_MODEL, F32)],
        compiler_params=_cparams("parallel"),
    )(o_sb, o_sw, gate, gate, gate, gate, h0, w_bsb, w_bsw, w_out)


def _loss_bwd(h1, target, gain2):
    L = h1.shape[0]

    def body(h_ref, t_ref, g_ref, dh_ref, loss_ref, dg_ref):
        i = pl.program_id(0)

        @pl.when(i == 0)
        def _():
            dh_ref[...] = jnp.zeros_like(dh_ref)
            loss_ref[...] = jnp.zeros_like(loss_ref)
            dg_ref[...] = jnp.zeros_like(dg_ref)

        @pl.when(i > 0)
        def _():
            h = h_ref[...]
            g = g_ref[...]
            r = lax.rsqrt(jnp.mean(h * h, axis=-1, keepdims=True) + RMS_EPS)
            n = h * r
            err = n * g - t_ref[...]
            loss_ref[...] += 0.5 * jnp.sum(jnp.mean(err * err, axis=-1, keepdims=True))
            dy = err * (1.0 / D_MODEL)
            dg_ref[...] += jnp.sum(dy * n, axis=0, keepdims=True)
            dn = dy * g
            dh_ref[...] = r * (dn - n * jnp.mean(dn * n, axis=-1, keepdims=True))

    return pl.pallas_call(
        body, name="loss_bwd", grid=(L // BLOCK,),
        in_specs=[pl.BlockSpec((BLOCK, D_MODEL), lambda i: (i, 0)),
                  pl.BlockSpec((BLOCK, D_MODEL), lambda i: (jnp.maximum(i - 1, 0), 0)),
                  pl.BlockSpec((1, D_MODEL), lambda i: (0, 0))],
        out_specs=[pl.BlockSpec((BLOCK, D_MODEL), lambda i: (i, 0)), pl.BlockSpec((1, BLOCK), lambda i: (0, 0)),
                   pl.BlockSpec((1, D_MODEL), lambda i: (0, 0))],
        out_shape=[jax.ShapeDtypeStruct((L, D_MODEL), F32), jax.ShapeDtypeStruct((1, BLOCK), F32),
                   jax.ShapeDtypeStruct((1, D_MODEL), F32)],
        compiler_params=_cparams("arbitrary"),
    )(h1, target, gain2)


def _post_bwd(dh1, gate, y_sb, y_sw, o_sb, o_sw, wt_out, wt_bsb, wt_bsw):
    L = dh1.shape[0]
    tm = _row_tile(L, 320)

    def body(dh, swz, gsb, gsw, sbz, ysb, ysw, osb, osw, wo, wb1, wb2, dy1_ref, dy2_ref, do1_ref, do2_ref, dg_ref):
        dm = _dot(dh[...].astype(BF16), wo[...])
        s1 = jax.nn.sigmoid(gsb[...])
        s2 = jax.nn.sigmoid(gsw[...])
        dy1 = (dm * s1).astype(BF16)
        dy2 = (dm * s2).astype(BF16)
        dy1_ref[...] = dy1
        dy2_ref[...] = dy2
        du1 = _dot(dy1, wb1[...])
        du2 = _dot(dy2, wb2[...])
        z1 = sbz[...]
        z2 = swz[...]
        sz1 = jax.nn.sigmoid(z1)
        sz2 = jax.nn.sigmoid(z2)
        do1_ref[...] = du1 * (z1 * sz1)
        do2_ref[...] = du2 * (z2 * sz2)
        dg_ref[:, 0:SW_W] = (du2 * osw[...] * (sz2 * (1.0 + z2 * (1.0 - sz2)))).astype(BF16)
        dg_ref[:, SW_W:SW_W + D_MODEL] = (dm * ysb[...] * (s1 * (1.0 - s1))).astype(BF16)
        dg_ref[:, SW_W + D_MODEL:SW_W + 2 * D_MODEL] = (dm * ysw[...] * (s2 * (1.0 - s2))).astype(BF16)
        dg_ref[:, SW_W + 2 * D_MODEL:GATE_W] = (du1 * osb[...] * (sz1 * (1.0 + z1 * (1.0 - sz1)))).astype(BF16)

    row = lambda w: pl.BlockSpec((tm, w), lambda i: (i, 0))
    whole = lambda a: pl.BlockSpec(a.shape, lambda i: (0, 0))
    shp = lambda w, dt: jax.ShapeDtypeStruct((L, w), dt)
    return pl.pallas_call(
        body, name="post_bwd", grid=(L // tm,),
        in_specs=[row(D_MODEL)] + _gate_specs(tm) + [row(D_MODEL), row(D_MODEL), row(SB_W), row(SW_W),
                                                     whole(wt_out), whole(wt_bsb), whole(wt_bsw)],
        out_specs=[row(D_MODEL), row(D_MODEL), row(SB_W), row(SW_W), row(GATE_W)],
        out_shape=[shp(D_MODEL, BF16), shp(D_MODEL, BF16), shp(SB_W, F32), shp(SW_W, F32), shp(GATE_W, BF16)],
        compiler_params=_cparams("parallel"),
    )(dh1, gate, gate, gate, gate, y_sb, y_sw, o_sb, o_sw, wt_out, wt_bsb, wt_bsw)


def _assemble_dqkv(dswq, dsbq, dsbk, dsbv, dkp, dkc, dvp, dvc, cos_t, sin_t):
    L = dswq.shape[0]
    tm = BLOCK
    nb = L // BLOCK

    def body(dq_ref, dsq_ref, dsk_ref, dsv_ref, dkp_ref, dkc_ref, dvp_ref, dvc_ref, c_ref, s_ref, o_ref):
        C = c_ref[...]
        S = s_ref[...]
        lane = lax.broadcasted_iota(jnp.int32, (tm, BLOCK), 1)
        lo32 = lane % HEAD_DIM < HEAD_DIM // 2
        lo = lane < HEAD_DIM
        has_next = (pl.program_id(0) + 1 < nb).astype(F32)

        def unrope(dy):
            return dy * C + _rope_partner(dy * S, lo32)

        def fold(cur_ref, next_ref):
            b00, b01, b10, b11 = (cur_ref[:, k * BLOCK:(k + 1) * BLOCK] + has_next * next_ref[:, k * BLOCK:(k + 1) * BLOCK]
                                  for k in range(4))
            return jnp.where(lo, b00 + pltpu.roll(b01, HEAD_DIM, 1), pltpu.roll(b10, HEAD_DIM, 1) + b11)

        for m in range(SW_W // BLOCK):
            sl = slice(m * BLOCK, (m + 1) * BLOCK)
            o_ref[:, sl] = unrope(dq_ref[:, sl] * Q_SCALE).astype(BF16)
        o_ref[:, SW_W:SW_W + SB_W] = (dsq_ref[...] * Q_SCALE).astype(BF16)
        o_ref[:, SW_W + SB_W:SW_W + 2 * SB_W] = dsk_ref[...].astype(BF16)
        o_ref[:, SW_W + 2 * SB_W:SW_W + 3 * SB_W] = dsv_ref[...].astype(BF16)
        base = SW_W + 3 * SB_W
        o_ref[:, base:base + KV_W] = unrope(fold(dkc_ref, dkp_ref)).astype(BF16)
        o_ref[:, base + KV_W:base + 2 * KV_W] = fold(dvc_ref, dvp_ref).astype(BF16)

    row = lambda w: pl.BlockSpec((tm, w), lambda i: (i, 0))
    nxt = pl.BlockSpec((tm, SB_W), lambda i: (jnp.minimum(i + 1, nb - 1), 0))
    return pl.pallas_call(
        body, name="assemble_dqkv", grid=(nb,),
        in_specs=[row(SW_W), row(SB_W), row(SB_W), row(SB_W), nxt, row(SB_W), nxt, row(SB_W), row(BLOCK), row(BLOCK)],
        out_specs=row(QKV_W), out_shape=jax.ShapeDtypeStruct((L, QKV_W), BF16),
        compiler_params=_cparams("parallel"),
    )(dswq, dsbq, dsbk, dsbv, dkp, dkc, dvp, dvc, cos_t, sin_t)


def _norm_bwd(dxn, h0, dh1, gain):
    L = h0.shape[0]

    def body(dx_ref, h_ref, dh_ref, g_ref, gx_ref, dm_ref, dg_ref):
        i = pl.program_id(0)

        @pl.when(i == 0)
        def _():
            dg_ref[...] = jnp.zeros_like(dg_ref)

        h = h_ref[...]
        dxn_t = dx_ref[...]
        r = lax.rsqrt(jnp.mean(h * h, axis=-1, keepdims=True) + RMS_EPS)
        n = h * r
        dg_ref[...] += jnp.sum(dxn_t * n, axis=0, keepdims=True)
        dn = dxn_t * g_ref[...]
        dh0 = dh_ref[...] + r * (dn - n * jnp.mean(dn * n, axis=-1, keepdims=True))
        gx_ref[...] = dh0

        @pl.when(i == 0)
        def _():
            dm_ref[...] = dh0[PAD:BLOCK]

    row = pl.BlockSpec((BLOCK, D_MODEL), lambda i: (i, 0))
    return pl.pallas_call(
        body, name="norm_bwd", grid=(L // BLOCK,),
        in_specs=[row, row, row, pl.BlockSpec((1, D_MODEL), lambda i: (0, 0))],
        out_specs=[pl.BlockSpec((BLOCK, D_MODEL), lambda i: (jnp.maximum(i - 1, 0), 0)),
                   pl.BlockSpec((N_META, D_MODEL), lambda i: (0, 0)), pl.BlockSpec((1, D_MODEL), lambda i: (0, 0))],
        out_shape=[jax.ShapeDtypeStruct((L - BLOCK, D_MODEL), F32), jax.ShapeDtypeStruct((N_META, D_MODEL), F32),
                   jax.ShapeDtypeStruct((1, D_MODEL), F32)],
        compiler_params=_cparams("arbitrary"),
    )(dxn, h0, dh1, gain)


def _adamw(w, g, m, v, name):
    R, C = w.shape
    tr = _row_tile(R, 256) if R % 16 == 0 else R

    def body(w_ref, g_ref, m_ref, v_ref, d_ref, nm_ref, nv_ref):
        g = g_ref[...]
        m_new = ADAM_B1 * m_ref[...] + (1.0 - ADAM_B1) * g
        v_new = ADAM_B2 * v_ref[...] + (1.0 - ADAM_B2) * (g * g)
        m_hat = m_new / (1.0 - ADAM_B1 ** ADAM_STEP)
        v_hat = v_new / (1.0 - ADAM_B2 ** ADAM_STEP)
        d_ref[...] = -ADAM_LR * (m_hat / (jnp.sqrt(v_hat) + ADAM_EPS) + ADAM_WD * w_ref[...])
        nm_ref[...] = m_new
        nv_ref[...] = v_new

    spec = pl.BlockSpec((tr, C), lambda i: (i, 0))
    shp = jax.ShapeDtypeStruct((R, C), F32)
    return pl.pallas_call(
        body, name=name, grid=(R // tr,), in_specs=[spec] * 4, out_specs=[spec] * 3, out_shape=[shp] * 3,
        compiler_params=_cparams("parallel"),
    )(w, g, m, v)


def _sum_leading(a, name):
    n, R, C = a.shape
    tr = _row_tile(R, 160) if R % 16 == 0 else R

    def body(a_ref, o_ref):
        acc = a_ref[0]
        for k in range(1, n):
            acc = acc + a_ref[k]
        o_ref[...] = acc

    return pl.pallas_call(
        body, name=name, grid=(R // tr,), in_specs=[pl.BlockSpec((n, tr, C), lambda i: (0, i, 0))],
        out_specs=pl.BlockSpec((tr, C), lambda i: (i, 0)), out_shape=jax.ShapeDtypeStruct((R, C), a.dtype),
        compiler_params=_cparams("parallel"),
    )(a)


def _place():
    x, y, c = lax.axis_index("x"), lax.axis_index("y"), lax.axis_index("c")
    return x, y, c


def _all_gather(block, name):
    R, C = block.shape

    def body(x_ref, out_ref, send_sems, recv_sems, local_sem):
        x, y, c = _place()
        me, sibling = (x, y, c), (x, y, 1 - c)
        chips = [(1 - x, y), (x, 1 - y), (1 - x, 1 - y)]

        def slot(px, py, pc):
            return out_ref.at[4 * px + 2 * py + pc]

        def copy(k, blk, to, src=None):
            return pltpu.make_async_remote_copy(
                src_ref=slot(*blk) if src is None else src, dst_ref=slot(*blk),
                send_sem=send_sems.at[k], recv_sem=recv_sems.at[k], device_id=to, device_id_type=MESH)

        mine = pltpu.make_async_copy(x_ref, slot(*me), local_sem)
        mine.start()
        first = [copy(0, me, sibling, src=x_ref)]
        first += [copy(1 + j, me, (*chip, c), src=x_ref) for j, chip in enumerate(chips)]
        for cp in first:
            cp.start()
        passed = [copy(4 + j, (*chip, c), sibling) for j, chip in enumerate(chips)]
        for j, chip in enumerate(chips):
            copy(1 + j, (*chip, c), me).wait_recv()
            passed[j].start()
        copy(0, sibling, me).wait_recv()
        for j, chip in enumerate(chips):
            copy(4 + j, (*chip, 1 - c), me).wait_recv()
        for cp in first + passed:
            cp.wait_send()
        mine.wait()

    return pl.pallas_call(
        body, name=name, in_specs=[ANY], out_specs=ANY,
        out_shape=jax.ShapeDtypeStruct((N_DEV, R, C), block.dtype),
        scratch_shapes=[pltpu.SemaphoreType.DMA((7,)), pltpu.SemaphoreType.DMA((7,)), pltpu.SemaphoreType.DMA],
    )(block)


def _pair_exchange(g8):
    _, R, C = g8.shape

    def body(g_ref, out_ref, send_sems, recv_sems):
        x, y, c = _place()
        sibling = (x, y, 1 - c)
        sends = [pltpu.make_async_remote_copy(
            src_ref=g_ref.at[2 * s + (1 - c)], dst_ref=out_ref.at[s], send_sem=send_sems.at[s],
            recv_sem=recv_sems.at[s], device_id=sibling, device_id_type=MESH) for s in range(N_CHIP)]
        for cp in sends:
            cp.start()
        for s in range(N_CHIP):
            pltpu.make_async_remote_copy(
                src_ref=g_ref.at[s], dst_ref=out_ref.at[s], send_sem=send_sems.at[s],
                recv_sem=recv_sems.at[s], device_id=sibling, device_id_type=MESH).wait_recv()
        for cp in sends:
            cp.wait_send()

    return pl.pallas_call(
        body, name="grad_pair_exchange", in_specs=[ANY], out_specs=ANY,
        out_shape=jax.ShapeDtypeStruct((N_CHIP, R, C), g8.dtype),
        scratch_shapes=[pltpu.SemaphoreType.DMA((N_CHIP,)), pltpu.SemaphoreType.DMA((N_CHIP,))],
    )(g8)


def _add_selected(a, b, a_index, name, wire_copy=False):
    select, sel_fn = a_index
    bs = b if isinstance(b, (list, tuple)) else [b]
    m, R, C = bs[0].shape
    tr = _row_tile(R, 160)
    n_out = 2 if wire_copy else 1

    def body(sel_ref, a_ref, *refs):
        acc = a_ref[...].astype(F32)
        for r in refs[:len(bs)]:
            acc = acc + r[...].astype(F32)
        refs[len(bs)][...] = acc
        if wire_copy:
            refs[len(bs) + 1][...] = acc.astype(BF16)

    blk = lambda fn: pl.BlockSpec((None, tr, C), fn)
    same = blk(lambda s, i, sel: (s, i, 0))
    out = pl.pallas_call(
        body, name=name,
        grid_spec=pltpu.PrefetchScalarGridSpec(
            num_scalar_prefetch=1, grid=(m, R // tr),
            in_specs=[blk(lambda s, i, sel: (sel_fn(s, sel[0]), i, 0))] + [same] * len(bs),
            out_specs=[same] * n_out),
        out_shape=[jax.ShapeDtypeStruct((m, R, C), F32), jax.ShapeDtypeStruct((m, R, C), BF16)][:n_out],
        compiler_params=_cparams("parallel", "parallel"),
    )(jnp.reshape(select, (1,)).astype(jnp.int32), a, *bs)
    return out if wire_copy else out[0]


def _chip_scatter(p4):
    _, R, C = p4.shape

    def body(p_ref, out_x, out_y, out_xy, send_sems, recv_sems):
        x, y, c = _place()
        chips = [(1 - x, y), (x, 1 - y), (1 - x, 1 - y)]
        outs = [out_x, out_y, out_xy]
        sends = [pltpu.make_async_remote_copy(
            src_ref=p_ref.at[2 * cx + cy], dst_ref=outs[j].at[0], send_sem=send_sems.at[j],
            recv_sem=recv_sems.at[j], device_id=(cx, cy, c), device_id_type=MESH) for j, (cx, cy) in enumerate(chips)]
        for cp in sends:
            cp.start()
        for j, (cx, cy) in enumerate(chips):
            pltpu.make_async_remote_copy(
                src_ref=p_ref.at[0], dst_ref=outs[j].at[0], send_sem=send_sems.at[j],
                recv_sem=recv_sems.at[j], device_id=(cx, cy, c), device_id_type=MESH).wait_recv()
        for cp in sends:
            cp.wait_send()

    shp = jax.ShapeDtypeStruct((1, R, C), p4.dtype)
    return pl.pallas_call(
        body, name="grad_chip_scatter", in_specs=[ANY], out_specs=[ANY, ANY, ANY], out_shape=[shp, shp, shp],
        scratch_shapes=[pltpu.SemaphoreType.DMA((3,)), pltpu.SemaphoreType.DMA((3,))],
    )(p4)


def _pair_share(r):
    R, C = r.shape

    def body(r_ref, out_ref, send_sem, recv_sem):
        x, y, c = _place()
        copy = pltpu.make_async_remote_copy(src_ref=r_ref, dst_ref=out_ref, send_sem=send_sem, recv_sem=recv_sem,
                                            device_id=(x, y, 1 - c), device_id_type=MESH)
        copy.start()
        copy.wait()

    return pl.pallas_call(
        body, name="grad_pair_share", in_specs=[ANY], out_specs=ANY,
        out_shape=jax.ShapeDtypeStruct((R, C), r.dtype),
        scratch_shapes=[pltpu.SemaphoreType.DMA, pltpu.SemaphoreType.DMA],
    )(r)


def _gather_direct(block, name):
    R, C = block.shape

    def body(x_ref, out_ref, send_sems, recv_sems, local_sem):
        x, y, c = _place()
        me = 4 * x + 2 * y + c
        local = pltpu.make_async_copy(x_ref, out_ref.at[me], local_sem)
        local.start()
        peers = []
        for k in range(1, N_DEV):
            px, py, pc = x ^ (k >> 2), y ^ ((k >> 1) & 1), c ^ (k & 1)
            peers.append((k, (px, py, pc), 4 * px + 2 * py + pc))
        sends = [pltpu.make_async_remote_copy(
            src_ref=x_ref, dst_ref=out_ref.at[me], send_sem=send_sems.at[k - 1], recv_sem=recv_sems.at[k - 1],
            device_id=dev, device_id_type=MESH) for k, dev, _ in peers]
        for cp in sends:
            cp.start()
        for k, dev, idx in peers:
            pltpu.make_async_remote_copy(
                src_ref=x_ref, dst_ref=out_ref.at[idx], send_sem=send_sems.at[k - 1], recv_sem=recv_sems.at[k - 1],
                device_id=dev, device_id_type=MESH).wait_recv()
        for cp in sends:
            cp.wait_send()
        local.wait()

    return pl.pallas_call(
        body, name=name, in_specs=[ANY], out_specs=ANY,
        out_shape=jax.ShapeDtypeStruct((N_DEV, R, C), block.dtype),
        scratch_shapes=[pltpu.SemaphoreType.DMA((7,)), pltpu.SemaphoreType.DMA((7,)), pltpu.SemaphoreType.DMA],
    )(block)


W_IN_SHARD = IN_COLS // N_CHIP
ROWS_W_IN = (D_MODEL // 2) * W_IN_SHARD // D_MODEL
ROWS_BSB = (SB_W // 2) * (D_MODEL // N_CHIP) // D_MODEL
ROWS_SQ = D_MODEL // N_CHIP // 2
ROWS_BIG = ROWS_W_IN + ROWS_BSB + 2 * ROWS_SQ


def _half(a, c, axis):
    n = a.shape[axis] // 2
    return lax.dynamic_slice_in_dim(a, c * n, n, axis)


def _pack_shard_half(w_in, w_bsb, w_bsw, w_out, c):
    parts = [_half(w_in, c, 0).reshape(ROWS_W_IN, D_MODEL), _half(w_bsb, c, 0).reshape(ROWS_BSB, D_MODEL),
             _half(w_bsw, c, 0), _half(w_out, c, 0)]
    return jnp.concatenate(parts, axis=0)


def _unpack_full(blocks):
    b = blocks.reshape(N_CHIP, 2, ROWS_BIG, D_MODEL)
    o = 0
    w_in = b[:, :, o:o + ROWS_W_IN].reshape(N_CHIP, D_MODEL, W_IN_SHARD)
    o += ROWS_W_IN
    w_bsb = b[:, :, o:o + ROWS_BSB].reshape(N_CHIP, SB_W, D_MODEL // N_CHIP)
    o += ROWS_BSB
    w_bsw = b[:, :, o:o + ROWS_SQ].reshape(D_MODEL, D_MODEL)
    o += ROWS_SQ
    w_out = b[:, :, o:o + ROWS_SQ].reshape(D_MODEL, D_MODEL)
    w_in = jnp.transpose(w_in, (1, 0, 2)).reshape(D_MODEL, IN_COLS)
    w_bsb = jnp.transpose(w_bsb, (1, 0, 2)).reshape(SB_W, D_MODEL)
    return w_in, w_bsb, w_bsw, w_out


def _pack_grads(dw_in, dw_bsb, dw_bsw, dw_out):
    a = jnp.transpose(dw_in.reshape(2, D_MODEL // 2, N_CHIP, W_IN_SHARD), (2, 0, 1, 3)).reshape(N_CHIP, 2, ROWS_W_IN, D_MODEL)
    b = jnp.transpose(dw_bsb.reshape(2, SB_W // 2, N_CHIP, D_MODEL // N_CHIP), (2, 0, 1, 3)).reshape(N_CHIP, 2, ROWS_BSB, D_MODEL)
    c = dw_bsw.reshape(N_CHIP, 2, ROWS_SQ, D_MODEL)
    d = dw_out.reshape(N_CHIP, 2, ROWS_SQ, D_MODEL)
    return jnp.concatenate([a, b, c, d], axis=2).reshape(N_DEV, ROWS_BIG, D_MODEL)


def _unpack_shard(full2):
    o = 0
    w_in = full2[:, o:o + ROWS_W_IN].reshape(D_MODEL, W_IN_SHARD)
    o += ROWS_W_IN
    w_bsb = full2[:, o:o + ROWS_BSB].reshape(SB_W, D_MODEL // N_CHIP)
    o += ROWS_BSB
    w_bsw = full2[:, o:o + ROWS_SQ].reshape(D_MODEL // N_CHIP, D_MODEL)
    o += ROWS_SQ
    w_out = full2[:, o:o + ROWS_SQ].reshape(D_MODEL // N_CHIP, D_MODEL)
    return w_in, w_bsb, w_bsw, w_out


_QKV_FROM_IN = ((1536, 2560), (0, 512), (512, 1024), (1024, 1536), (2560, 2688), (2688, 2816))
_GATE_FROM_IN = ((3328, 4352), (4352, 5376), (5376, 6400), (2816, 3328))


def _split_w_in(w_in):
    qkv = jnp.concatenate([w_in[:, a:b] for a, b in _QKV_FROM_IN], axis=1)
    gate = jnp.concatenate([w_in[:, a:b] for a, b in _GATE_FROM_IN], axis=1)
    return qkv, gate


def _join_dw_in(dqkv, dgate):
    q = lambda a, b: dqkv[:, a:b]
    g = lambda a, b: dgate[:, a:b]
    return jnp.concatenate([q(1024, 1536), q(1536, 2048), q(2048, 2560), q(0, 1024), q(2560, 2688), q(2688, 2816),
                            g(3072, 3584), g(0, 1024), g(1024, 2048), g(2048, 3072)], axis=1)


def _rope_tables(L):
    half = HEAD_DIM // 2
    inv = ROPE_THETA ** (-jnp.arange(half, dtype=F32) / half)
    pos = (jnp.arange(L) - PAD).astype(F32)
    ang = pos[:, None] * inv[None, :]
    cos, sin = jnp.cos(ang), jnp.sin(ang)
    return jnp.tile(jnp.concatenate([cos, cos], axis=1), (1, 2)), jnp.tile(jnp.concatenate([-sin, sin], axis=1), (1, 2))


def kernel(x, meta_tokens, norm_gain, w_in, w_branch_sb, w_branch_swa, w_out, attn_sinks, final_norm_gain, loss_target, m_meta_tokens, m_norm_gain, m_w_in, m_w_branch_sb, m_w_branch_swa, m_w_out, m_attn_sinks, m_final_norm_gain, v_meta_tokens, v_norm_gain, v_w_in, v_w_branch_sb, v_w_branch_swa, v_w_out, v_attn_sinks, v_final_norm_gain):
    xi, yi, ci = _place()
    chip = 2 * xi + yi
    seq = x.shape[1]
    L = seq + BLOCK

    wpack = _pack_shard_half(w_in[0], w_branch_sb[0], w_branch_swa[0], w_out[0], ci).astype(BF16)
    w_in_f, w_bsb_f, w_bsw_f, w_out_f = _unpack_full(_all_gather(wpack, "weight_all_gather"))
    mg = _all_gather(_half(meta_tokens, ci, 0), "meta_all_gather").reshape(N_CHIP, 2, N_META // 2, D_MODEL // N_CHIP)
    meta_full = jnp.transpose(mg, (1, 2, 0, 3)).reshape(N_META, D_MODEL)
    w_qkv, w_gate = _split_w_in(w_in_f)

    h0 = jnp.concatenate([jnp.zeros((PAD, D_MODEL), F32), meta_full, x[0]], axis=0)
    cos_t, sin_t = _rope_tables(L)
    xn = _norm_fwd(h0, norm_gain)
    qkv = _mm([(xn, w_qkv)], F32, "in_proj_qkv", tn_pref=1408)
    gate = _mm([(xn, w_gate)], F32, "in_proj_gate")
    swq, sbq, kcat, vcat, swk4, swv4 = _prep_qkv(qkv, cos_t, sin_t)
    o_sb = _sb_fwd(sbq, kcat, vcat)
    o_sw = _swa_fwd(swq, swk4, swv4, attn_sinks)
    h1, u_sb, u_sw, merged, y_sb, y_sw = _post_fwd(o_sb, o_sw, gate, h0, w_bsb_f, w_bsw_f, w_out_f)

    dh1, loss_part, dgain2 = _loss_bwd(h1, loss_target[0], final_norm_gain.reshape(1, D_MODEL))
    dy_sb, dy_sw, do_sb, do_sw, dgate = _post_bwd(dh1, gate, y_sb, y_sw, o_sb, o_sw, w_out_f.T, w_bsb_f.T, w_bsw_f.T)
    dsbq, dsbk, dsbv = _sb_bwd(sbq, kcat, vcat, do_sb)
    dswq, dkp, dkc, dvp, dvc, dsink = _swa_bwd(swq, swk4, swv4, attn_sinks, do_sw)
    dqkv = _assemble_dqkv(dswq, dsbq, dsbk, dsbv, dkp, dkc, dvp, dvc, cos_t, sin_t)
    dxn = _mm([(dqkv, w_qkv.T), (dgate, w_gate.T)], F32, "dxn", tm_pref=320, tn_pref=1024)
    grad_x, dmeta, dgain1 = _norm_bwd(dxn, h0, dh1, norm_gain)
    dw_qkv = _mm_tn(xn, dqkv, "dw_qkv", tn_pref=1408)
    dw_gate = _mm_tn(xn, dgate, "dw_gate")
    dw_bsb = _mm_tn(u_sb, dy_sb, "dw_branch_sb", tn_pref=1024)
    dw_bsw = _mm_tn(u_sw, dy_sw, "dw_branch_swa", tn_pref=1024)
    dw_out = _mm_tn(merged, dh1, "dw_out", tn_pref=1024)

    g8 = _pack_grads(_join_dw_in(dw_qkv, dw_gate), dw_bsb, dw_bsw, dw_out).astype(BF16)
    p4, p4_wire = _add_selected(g8, _pair_exchange(g8), (ci, lambda s, c: 2 * s + c), "grad_pair_sum", wire_copy=True)
    r_half = _add_selected(p4, list(_chip_scatter(p4_wire)), (chip, lambda s, own: own), "grad_chip_sum")[0]
    r_sib = _pair_share(r_half)
    full2 = jnp.where(ci == 0, jnp.stack([r_half, r_sib]), jnp.stack([r_sib, r_half]))
    g_in, g_bsb, g_bsw, g_out = _unpack_shard(full2)

    small = jnp.concatenate([dmeta, dgain1, dgain2, jnp.pad(dsink[0:1], ((0, 0), (0, D_MODEL - BLOCK))),
                             jnp.pad(loss_part, ((0, 0), (0, D_MODEL - BLOCK))),
                             jnp.zeros((32 - N_META - 4, D_MODEL), F32)], axis=0)
    tot = _sum_leading(_gather_direct(small, "small_all_gather"), "small_sum")
    g_meta = lax.dynamic_slice_in_dim(tot[0:N_META], chip * (D_MODEL // N_CHIP), D_MODEL // N_CHIP, 1)
    g_gain1 = tot[N_META:N_META + 1]
    g_gain2 = tot[N_META + 1]
    g_sinks = tot[N_META + 2:N_META + 3, 0:16]
    loss = tot[N_META + 3, 0]

    def upd(w, g, m, v, name):
        shape = w.shape
        as2d = lambda a: a.reshape(-1, shape[-1])
        d, nm, nv = _adamw(as2d(w), as2d(g), as2d(m), as2d(v), name)
        return g.reshape(shape), d.reshape(shape), nm.reshape(shape), nv.reshape(shape)

    res = [
        upd(meta_tokens, g_meta, m_meta_tokens, v_meta_tokens, "adamw_meta"),
        upd(norm_gain, g_gain1, m_norm_gain, v_norm_gain, "adamw_norm_gain"),
        upd(w_in, g_in, m_w_in, v_w_in, "adamw_w_in"),
        upd(w_branch_sb, g_bsb, m_w_branch_sb, v_w_branch_sb, "adamw_w_branch_sb"),
        upd(w_branch_swa, g_bsw, m_w_branch_swa, v_w_branch_swa, "adamw_w_branch_swa"),
        upd(w_out, g_out, m_w_out, v_w_out, "adamw_w_out"),
        upd(attn_sinks, g_sinks, m_attn_sinks, v_attn_sinks, "adamw_attn_sinks"),
        upd(final_norm_gain, g_gain2, m_final_norm_gain, v_final_norm_gain, "adamw_final_norm_gain"),
    ]
    grads, deltas, new_m, new_v = zip(*res)
    return (loss, grad_x.reshape(1, seq, D_MODEL), *grads, *deltas, *new_m, *new_v)
```
